```python
import jax, jax.numpy as jnp
from jax import lax
import numpy as np


D_MODEL = 1024
BATCH = 32
SEQ = 256
DEPTH = 2
DEC_BATCH = 4
DEC_SEQ = 4096
PAST_LEN = 512

GRID_W = 64
N_MOD = 9
D_FF = 2816
EPS = 1e-6
POOL_GROUPS = 4
POOL_GW = 64
POOL_W = POOL_GROUPS * POOL_GW
POOL_WINDOWS = (2, 4, 8, 16)
SGU_GROUPS = 4
SGU_GW = 64
SGU_W = SGU_GROUPS * SGU_GW
CHUNK = 128
MLA_HEADS = 8
QK_NOPE = 64
QK_ROPE = 32
V_HEAD = 64
Q_LORA = 256
KV_LORA = 128
MLA_W = MLA_HEADS * V_HEAD
ROPE_THETA = 10000.0
Q_BLOCK = 128
CONV_W = 256
CONV_K = 3
N_BRANCH = 4
IN_WIDTHS = (POOL_W, SGU_W, SGU_W, Q_LORA, KV_LORA, QK_ROPE, CONV_W, CONV_W, CONV_W)
IN_SPLITS = tuple(int(v) for v in np.cumsum(IN_WIDTHS)[:-1])
D_IN = int(sum(IN_WIDTHS))

kernel_name = 'hybrid_diffusion_parallel_mixer_step'


def rmsnorm(x, g):
    xf = x.astype(jnp.float32)
    y = xf * lax.rsqrt(jnp.mean(xf * xf, axis=-1, keepdims=True) + EPS)
    return (y * g.astype(jnp.float32)).astype(x.dtype)


def swiglu(h, w_gu, w_dn):
    g, u = jnp.split(h @ w_gu, 2, axis=-1)
    return (jax.nn.silu(g) * u) @ w_dn


def centred_window_mean(x, w):
    L = x.shape[1]
    cs = jnp.pad(jnp.cumsum(x, axis=1), ((0, 0), (1, 0), (0, 0)))
    t = jnp.arange(L)
    lo = jnp.clip(t - w // 2, 0, L)
    hi = jnp.clip(t + w - w // 2, 0, L)
    s = jnp.take(cs, hi, axis=1) - jnp.take(cs, lo, axis=1)
    cnt = (hi - lo).astype(jnp.float32)
    return s / cnt[None, :, None]


def pool_mixer(xp, w_pool, scale):
    xf = xp.astype(jnp.float32)
    outs = []
    for g, w in enumerate(POOL_WINDOWS):
        xg = xf[..., g * POOL_GW:(g + 1) * POOL_GW]
        outs.append(centred_window_mean(xg, w) - xg)
    d = jnp.stack(outs, axis=-2).astype(xp.dtype)
    y = jnp.einsum('blgc,gcd->blgd', d, w_pool)
    return y.reshape(xp.shape) * scale


def spatial_gating(u, v, g_norm, w_s, b_s):
    B, L, _ = v.shape
    vc = rmsnorm(v, g_norm).reshape(B, L // CHUNK, CHUNK, SGU_GROUPS, SGU_GW)
    mixed = jnp.einsum('gpq,bnqgc->bnpgc', w_s, vc) + b_s.T[None, None, :, :, None]
    return u * mixed.reshape(B, L, SGU_W)


def short_conv(x, w):
    return lax.conv_general_dilated(x, w[:, None, :], window_strides=(1,),
                                    padding=((CONV_K // 2, CONV_K // 2),),
                                    dimension_numbers=('NWC', 'WIO', 'NWC'),
                                    feature_group_count=CONV_W)


def axial_rope_tables(n):
    rows = n // GRID_W
    r = jnp.broadcast_to(jnp.arange(rows)[:, None], (rows, GRID_W)).reshape(-1).astype(jnp.float32)
    col = jnp.broadcast_to(jnp.arange(GRID_W)[None, :], (rows, GRID_W)).reshape(-1).astype(jnp.float32)
    half = QK_ROPE // 2
    freqs = ROPE_THETA ** (-(2.0 * jnp.arange(half // 2, dtype=jnp.float32)) / half)
    ang = jnp.stack([r[:, None] * freqs, col[:, None] * freqs], axis=1)
    return jnp.cos(ang), jnp.sin(ang)


def apply_axial_rope(x, cos, sin):
    shp = x.shape
    xr = x.reshape(shp[:-1] + (2, 2, QK_ROPE // 4)).astype(jnp.float32)
    x1, x2 = xr[..., 0, :], xr[..., 1, :]
    c = cos[None, :, None]
    s = sin[None, :, None]
    out = jnp.stack([x1 * c - x2 * s, x1 * s + x2 * c], axis=-2)
    return out.reshape(shp).astype(x.dtype)


def mla_kv(ckv_n, kr, w_ukv):
    B, L, _ = ckv_n.shape
    kv = (ckv_n @ w_ukv).reshape(B, L, MLA_HEADS, QK_NOPE + V_HEAD)
    k_nope, v = kv[..., :QK_NOPE], kv[..., QK_NOPE:]
    k = jnp.concatenate([k_nope, jnp.broadcast_to(kr[:, :, None, :], (B, L, MLA_HEADS, QK_ROPE))], axis=-1)
    return k, v


def attend(q, k, v):
    B, Lq, H, Dk = q.shape
    nb = Lq // Q_BLOCK
    qb = q.reshape(B, nb, Q_BLOCK, H, Dk).transpose(1, 0, 2, 3, 4)
    scale = Dk ** -0.5

    def one(qblk):
        s = jnp.einsum('bqhd,bkhd->bhqk', qblk, k).astype(jnp.float32) * scale
        p = jax.nn.softmax(s, axis=-1).astype(v.dtype)
        return jnp.einsum('bhqk,bkhd->bqhd', p, v)

    o = lax.map(one, qb)
    return o.transpose(1, 0, 2, 3, 4).reshape(B, Lq, H, v.shape[-1])


def token_mix(h, l, P, ctx, rope):
    B, L, _ = h.shape
    z = h @ P['w_in'][l]
    zp, zu, zv, cq, ckv, kr, zb, zc, zx = jnp.split(z, IN_SPLITS, axis=-1)
    a = pool_mixer(zp, P['w_pool'][l], P['pool_scale'][l])
    b = spatial_gating(zu, zv, P['g_sgu'][l], P['w_sgu'][l], P['b_sgu'][l])
    q = (rmsnorm(cq, P['g_q'][l]) @ P['w_uq'][l]).reshape(B, L, MLA_HEADS, QK_NOPE + QK_ROPE)
    ckv_n = rmsnorm(ckv, P['g_kv'][l])
    if ctx is None:
        k, v = mla_kv(ckv_n, kr, P['w_ukv'][l])
        att = attend(q, k, v)
        state = (ckv_n, kr)
    else:
        cos, sin = rope
        q = jnp.concatenate([q[..., :QK_NOPE], apply_axial_rope(q[..., QK_NOPE:], cos, sin)], axis=-1)
        kr_rot = apply_axial_rope(kr[:, :, None, :], cos, sin)[:, :, 0, :]
        k_lat, v_lat = mla_kv(ckv_n, kr_rot, P['w_ukv'][l])
        k_ctx, v_ctx = mla_kv(ctx[0], ctx[1], P['w_ukv'][l])
        att = attend(q, jnp.concatenate([k_ctx, k_lat], axis=1), jnp.concatenate([v_ctx, v_lat], axis=1))
        state = None
    att = att.reshape(B, L, MLA_W)
    dconv = zb * short_conv(zc * zx, P['conv_w'][l])
    gates = jax.nn.sigmoid(h @ P['w_gate'][l] + P['b_gate'][l]).reshape(B, L, N_BRANCH, D_MODEL)
    merged = (gates[:, :, 0] * (a @ P['w_br_pool'][l])
              + gates[:, :, 1] * (b @ P['w_br_sgu'][l])
              + gates[:, :, 2] * (att @ P['w_br_mla'][l])
              + gates[:, :, 3] * (dconv @ P['w_br_conv'][l]))
    return merged @ P['w_o'][l], state


def layer(x, cond, l, P, ctx, rope):
    Bc = cond.shape[0]
    mods = (jax.nn.silu(cond) @ P['w_mod'][l] + P['b_mod'][l]).reshape(Bc, N_MOD, 1, D_MODEL)

    def pre(x_, s):
        return rmsnorm(x_, P['g_pre'][l, s]) * (1 + mods[:, 3 * s + 1]) + mods[:, 3 * s]

    f = swiglu(pre(x, 0), P['w_ffn_gu'][l, 0], P['w_ffn_dn'][l, 0])
    x = x + 0.5 * mods[:, 2] * rmsnorm(f, P['g_post'][l, 0])
    m, state = token_mix(pre(x, 1), l, P, ctx, rope)
    x = x + mods[:, 5] * rmsnorm(m, P['g_post'][l, 1])
    f = swiglu(pre(x, 2), P['w_ffn_gu'][l, 1], P['w_ffn_dn'][l, 1])
    x = x + 0.5 * mods[:, 8] * rmsnorm(f, P['g_post'][l, 2])
    return x, state


def setup_inputs(seed: int = 0) -> dict:
    key = jax.random.key(seed)
    ks = iter(jax.random.split(key, 40))

    def nrm(shape, scale):
        return jax.random.normal(next(ks), shape, jnp.float32) * scale

    def gain(shape):
        return 1.0 + nrm(shape, 0.02)

    D, L = D_MODEL, DEPTH
    return {
        'x_prompt': nrm((BATCH, SEQ, D), 1.0),
        'x_sample': nrm((DEC_BATCH, DEC_SEQ, D), 1.0),
        'cache_ckv': nrm((DEC_BATCH, DEPTH, PAST_LEN, KV_LORA), 1.0),
        'cache_krope': nrm((DEC_BATCH, DEPTH, PAST_LEN, QK_ROPE), 1.0),
        'c': nrm((DEC_BATCH, D), 1.0),
        'c_ctx': nrm((D,), 1.0),
        'w_mod': nrm((L, D, N_MOD * D), 0.5 * D ** -0.5),
        'b_mod': nrm((L, N_MOD * D), 0.01),
        'g_pre': gain((L, 3, D)),
        'g_post': gain((L, 3, D)),
        'w_ffn_gu': nrm((L, 2, D, 2 * D_FF), D ** -0.5),
        'w_ffn_dn': nrm((L, 2, D_FF, D), D_FF ** -0.5),
        'w_in': nrm((L, D, D_IN), D ** -0.5),
        'w_pool': nrm((L, POOL_GROUPS, POOL_GW, POOL_GW), POOL_GW ** -0.5),
        'pool_scale': gain((L, POOL_W)),
        'g_sgu': gain((L, SGU_W)),
        'w_sgu': nrm((L, SGU_GROUPS, CHUNK, CHUNK), CHUNK ** -0.5),
        'b_sgu': 1.0 + nrm((L, SGU_GROUPS, CHUNK), 0.01),
        'g_q': gain((L, Q_LORA)),
        'w_uq': nrm((L, Q_LORA, MLA_HEADS * (QK_NOPE + QK_ROPE)), Q_LORA ** -0.5),
        'g_kv': gain((L, KV_LORA)),
        'w_ukv': nrm((L, KV_LORA, MLA_HEADS * (QK_NOPE + V_HEAD)), KV_LORA ** -0.5),
        'conv_w': nrm((L, CONV_K, CONV_W), CONV_K ** -0.5),
        'w_br_pool': nrm((L, POOL_W, D), POOL_W ** -0.5),
        'w_br_sgu': nrm((L, SGU_W, D), SGU_W ** -0.5),
        'w_br_mla': nrm((L, MLA_W, D), MLA_W ** -0.5),
        'w_br_conv': nrm((L, CONV_W, D), CONV_W ** -0.5),
        'w_gate': nrm((L, D, N_BRANCH * D), D ** -0.5),
        'b_gate': nrm((L, N_BRANCH * D), 0.01),
        'w_o': nrm((L, D, D), D ** -0.5),
    }


def reference(x_prompt, x_sample, cache_ckv, cache_krope, c, c_ctx, w_mod, b_mod, g_pre, g_post,
              w_ffn_gu, w_ffn_dn, w_in, w_pool, pool_scale, g_sgu, w_sgu, b_sgu, g_q, w_uq, g_kv,
              w_ukv, conv_w, w_br_pool, w_br_sgu, w_br_mla, w_br_conv, w_gate, b_gate, w_o):
    P = dict(w_mod=w_mod, b_mod=b_mod, g_pre=g_pre, g_post=g_post, w_ffn_gu=w_ffn_gu,
             w_ffn_dn=w_ffn_dn, w_in=w_in, w_pool=w_pool, pool_scale=pool_scale, g_sgu=g_sgu,
             w_sgu=w_sgu, b_sgu=b_sgu, g_q=g_q, w_uq=w_uq, g_kv=g_kv, w_ukv=w_ukv, conv_w=conv_w,
             w_br_pool=w_br_pool, w_br_sgu=w_br_sgu, w_br_mla=w_br_mla, w_br_conv=w_br_conv,
             w_gate=w_gate, b_gate=b_gate, w_o=w_o)
    x = x_prompt
    ckv_states, kr_states = [], []
    for l in range(DEPTH):
        x, st = layer(x, c_ctx[None, :], l, P, None, None)
        ckv_states.append(st[0])
        kr_states.append(st[1])
    y_prompt = x
    state_ckv = jnp.stack(ckv_states, axis=1)
    state_krope = jnp.stack(kr_states, axis=1)
    cos, sin = axial_rope_tables(x_sample.shape[1])
    x = x_sample
    for l in range(DEPTH):
        x, _ = layer(x, c, l, P, (cache_ckv[:, l], cache_krope[:, l]), (cos, sin))
    y_sample = x
    return (y_prompt, y_sample, state_ckv, state_krope)
```

```python
import functools

import numpy as np
import jax
import jax.numpy as jnp
from jax import lax
from jax.experimental import pallas as pl
from jax.experimental.pallas import tpu as pltpu

F32 = jnp.float32
BF16 = jnp.bfloat16

D_MODEL = 1024
DEPTH = 2
GRID_W = 64
N_MOD = 9
D_FF = 2816
EPS = 1e-6
POOL_GROUPS = 4
POOL_GW = 64
POOL_W = 256
SGU_GROUPS = 4
SGU_W = 256
CHUNK = 128
MLA_HEADS = 8
QK_NOPE = 64
QK_ROPE = 32
V_HEAD = 64
Q_LORA = 256
KV_LORA = 128
MLA_W = MLA_HEADS * V_HEAD
ROPE_THETA = 10000.0
CONV_W = 256
N_BRANCH = 4
HEAD_PAD = 128
QK_PAD = MLA_HEADS * HEAD_PAD
LOC_W = 6 * 256
MAIN_W = LOC_W + Q_LORA + KV_LORA
HALO = 8
POOL_HALF = (1, 2, 4, 8)

VMEM_LIMIT = 56 * 1024 * 1024

TM_TOKEN = 512
TM_LOCAL = 1024
TQ_ATTN = 256


def _const_spec(shape):
    n = len(shape)
    return pl.BlockSpec(shape, lambda *_: (0,) * n, pipeline_mode=pl.Buffered(1))


def _params(sem):
    return pltpu.CompilerParams(dimension_semantics=sem, vmem_limit_bytes=VMEM_LIMIT)


def _sigmoid(x):
    return 1.0 / (1.0 + jnp.exp(-x))


def _rms(x):
    return x * lax.rsqrt(jnp.mean(x * x, axis=-1, keepdims=True) + EPS)


def _pre(x, g, mods_ref, s):
    return (_rms(x) * g) * (1.0 + mods_ref[3 * s + 1:3 * s + 2, :]) + mods_ref[3 * s:3 * s + 1, :]


def _dot(a, b):
    return jnp.dot(a, b, preferred_element_type=F32)


def _mods_kernel(cond_ref, w_ref, b_ref, o_ref):
    c = cond_ref[...]
    a = (c * _sigmoid(c)).astype(BF16)
    o_ref[0] = _dot(a, w_ref[0].astype(BF16)) + b_ref[0]


def _mods_call(cond8, w_mod, b_mod):
    n_tile = 1024
    nt = (N_MOD * D_MODEL) // n_tile
    return pl.pallas_call(
        _mods_kernel,
        grid=(DEPTH, nt),
        in_specs=[
            pl.BlockSpec((8, D_MODEL), lambda l, j: (0, 0)),
            pl.BlockSpec((1, D_MODEL, n_tile), lambda l, j: (l, 0, j)),
            pl.BlockSpec((1, 1, n_tile), lambda l, j: (l, 0, j)),
        ],
        out_specs=pl.BlockSpec((1, 8, n_tile), lambda l, j: (l, 0, j)),
        out_shape=jax.ShapeDtypeStruct((DEPTH, 8, N_MOD * D_MODEL), F32),
        compiler_params=_params(("parallel", "parallel")),
        name="adaln_mods",
    )(cond8, w_mod, b_mod.reshape(DEPTH, 1, N_MOD * D_MODEL))


def _ffn_kernel(x_ref, mods_ref, gpre_ref, gpost_ref, wgu_ref, wdn_ref, o_ref, *, s):
    x = x_ref[...]
    h = _pre(x, gpre_ref[...], mods_ref, s).astype(BF16)
    g = _dot(h, wgu_ref[:, :D_FF])
    u = _dot(h, wgu_ref[:, D_FF:])
    a = ((g * _sigmoid(g)) * u).astype(BF16)
    f = _dot(a, wdn_ref[...])
    gate = mods_ref[3 * s + 2:3 * s + 3, :]
    o_ref[...] = x + (0.5 * gate) * (_rms(f) * gpost_ref[...])


def _ffn_call(x, mods, gpre, gpost, wgu, wdn, s):
    rows = x.shape[0]
    tm = TM_TOKEN
    per_cond = rows // mods.shape[0] // tm
    return pl.pallas_call(
        functools.partial(_ffn_kernel, s=s),
        grid=(rows // tm,),
        in_specs=[
            pl.BlockSpec((tm, D_MODEL), lambda i: (i, 0)),
            pl.BlockSpec((None, N_MOD, D_MODEL), lambda i: (i // per_cond, 0, 0)),
            _const_spec((1, D_MODEL)),
            _const_spec((1, D_MODEL)),
            _const_spec((D_MODEL, 2 * D_FF)),
            _const_spec((D_FF, D_MODEL)),
        ],
        out_specs=pl.BlockSpec((tm, D_MODEL), lambda i: (i, 0)),
        out_shape=jax.ShapeDtypeStruct((rows, D_MODEL), F32),
        compiler_params=_params(("parallel",)),
        name="ffn_half_step",
    )(x, mods, gpre, gpost, wgu, wdn)


def _front_kernel(*refs, rope):
    if rope:
        (x_ref, mods_ref, gpre_ref, win_ref, gq_ref, wq_ref, gkv_ref, wk_ref, wv_ref,
         cos_ref, sin_ref, zloc_ref, q_ref, k_ref, v_ref) = refs
    else:
        (x_ref, mods_ref, gpre_ref, win_ref, gq_ref, wq_ref, gkv_ref, wk_ref, wv_ref,
         zloc_ref, q_ref, k_ref, v_ref, ckv_ref, kr_ref) = refs
    h = _pre(x_ref[...], gpre_ref[...], mods_ref, 1).astype(BF16)
    z = _dot(h, win_ref[...])
    zloc_ref[...] = z[:, :LOC_W]
    qn = (_rms(z[:, LOC_W:LOC_W + Q_LORA]) * gq_ref[...]).astype(BF16)
    ckv_n = _rms(z[:, LOC_W + Q_LORA:MAIN_W]) * gkv_ref[...]
    kr = z[:, MAIN_W:MAIN_W + HEAD_PAD]
    q = _dot(qn, wq_ref[:, :QK_PAD])
    if rope:
        cos = cos_ref[...]
        sin = sin_ref[...]
        q_sw = _dot(qn, wq_ref[:, QK_PAD:])
        kr = kr * cos + z[:, MAIN_W + HEAD_PAD:] * sin
    else:
        ckv_ref[...] = ckv_n
        kr_ref[...] = kr[:, QK_NOPE:QK_NOPE + QK_ROPE]
    ckv_b = ckv_n.astype(BF16)
    kn = _dot(ckv_b, wk_ref[...])
    v_ref[...] = _dot(ckv_b, wv_ref[...]).astype(BF16)
    qk_scale = float(QK_NOPE + QK_ROPE) ** -0.5
    for hd in range(MLA_HEADS):
        sl = slice(hd * HEAD_PAD, (hd + 1) * HEAD_PAD)
        qh = q[:, sl]
        if rope:
            qh = qh * cos + q_sw[:, sl] * sin
        q_ref[:, sl] = (qh * qk_scale).astype(BF16)
        k_ref[:, sl] = (kn[:, sl] + kr).astype(BF16)


def _front_call(x, mods, gpre, win, gq, wq, gkv, wk, wv, rope_tabs):
    rows = x.shape[0]
    tm = TM_TOKEN
    per_cond = rows // mods.shape[0] // tm
    rope = rope_tabs is not None
    row_spec = lambda w: pl.BlockSpec((tm, w), lambda i: (i, 0))
    in_specs = [
        row_spec(D_MODEL),
        pl.BlockSpec((None, N_MOD, D_MODEL), lambda i: (i // per_cond, 0, 0)),
        _const_spec((1, D_MODEL)),
        _const_spec(win.shape),
        _const_spec((1, Q_LORA)),
        _const_spec(wq.shape),
        _const_spec((1, KV_LORA)),
        _const_spec(wk.shape),
        _const_spec(wv.shape),
    ]
    args = [x, mods, gpre, win, gq, wq, gkv, wk, wv]
    out_specs = [row_spec(LOC_W), row_spec(QK_PAD), row_spec(QK_PAD), row_spec(MLA_W)]
    out_shape = [
        jax.ShapeDtypeStruct((rows, LOC_W), F32),
        jax.ShapeDtypeStruct((rows, QK_PAD), BF16),
        jax.ShapeDtypeStruct((rows, QK_PAD), BF16),
        jax.ShapeDtypeStruct((rows, MLA_W), BF16),
    ]
    if rope:
        seq_tiles = rope_tabs[0].shape[0] // tm
        tab_spec = pl.BlockSpec((tm, HEAD_PAD), lambda i: (i % seq_tiles, 0))
        in_specs += [tab_spec, tab_spec]
        args += list(rope_tabs)
    else:
        out_specs += [row_spec(KV_LORA), row_spec(QK_ROPE)]
        out_shape += [jax.ShapeDtypeStruct((rows, KV_LORA), F32),
                      jax.ShapeDtypeStruct((rows, QK_ROPE), F32)]
    return pl.pallas_call(
        functools.partial(_front_kernel, rope=rope),
        grid=(rows // tm,),
        in_specs=in_specs,
        out_specs=out_specs,
        out_shape=out_shape,
        compiler_params=_params(("parallel",)),
        name="mix_front_rope" if rope else "mix_front",
    )(*args)


def _kvup_kernel(ckv_ref, kr_ref, wk_ref, wv_ref, place_ref, k_ref, v_ref):
    ckv_b = ckv_ref[...].astype(BF16)
    kn = _dot(ckv_b, wk_ref[...])
    kr = _dot(kr_ref[...].astype(BF16), place_ref[...])
    v_ref[...] = _dot(ckv_b, wv_ref[...]).astype(BF16)
    for hd in range(MLA_HEADS):
        sl = slice(hd * HEAD_PAD, (hd + 1) * HEAD_PAD)
        k_ref[:, sl] = (kn[:, sl] + kr).astype(BF16)


def _kvup_call(ckv, kr, wk, wv, place):
    b, n, _ = ckv.shape
    return pl.pallas_call(
        _kvup_kernel,
        grid=(b,),
        in_specs=[
            pl.BlockSpec((None, n, KV_LORA), lambda i: (i, 0, 0)),
            pl.BlockSpec((None, n, QK_ROPE), lambda i: (i, 0, 0)),
            _const_spec(wk.shape),
            _const_spec(wv.shape),
            _const_spec(place.shape),
        ],
        out_specs=[pl.BlockSpec((None, n, QK_PAD), lambda i: (i, 0, 0)),
                   pl.BlockSpec((None, n, MLA_W), lambda i: (i, 0, 0))],
        out_shape=[jax.ShapeDtypeStruct((b, n, QK_PAD), BF16),
                   jax.ShapeDtypeStruct((b, n, MLA_W), BF16)],
        compiler_params=_params(("parallel",)),
        name="cache_kv_up",
    )(ckv, kr, wk, wv, place)


def _local_kernel(z_ref, zprev_ref, znext_ref, wpool_ref, pscale_ref, gsgu_ref, wsgu_ref,
                  bsgu_ref, convw_ref, o_ref, ext_ref, *, seqlen, tm):
    i = pl.program_id(0)
    t = (lax.broadcasted_iota(jnp.int32, (tm, 1), 0) + i * tm) & (seqlen - 1)
    lane = lax.broadcasted_iota(jnp.int32, (1, POOL_W), 1)
    grp = lane // POOL_GW

    zp = z_ref[:, 0:256]
    zu = z_ref[:, 256:512]
    zv = z_ref[:, 512:768]
    zb = z_ref[:, 768:1024]
    prod = z_ref[:, 1024:1280] * z_ref[:, 1280:1536]

    ext_ref[0:HALO, 0:256] = zprev_ref[:, 0:256]
    ext_ref[0:HALO, 256:512] = zprev_ref[:, 1024:1280] * zprev_ref[:, 1280:1536]
    ext_ref[HALO:HALO + tm, 0:256] = zp
    ext_ref[HALO:HALO + tm, 256:512] = prod
    ext_ref[HALO + tm:, 0:256] = znext_ref[:, 0:256]
    ext_ref[HALO + tm:, 256:512] = znext_ref[:, 1024:1280] * znext_ref[:, 1280:1536]

    half = jnp.where(grp == 0, POOL_HALF[0],
                     jnp.where(grp == 1, POOL_HALF[1],
                               jnp.where(grp == 2, POOL_HALF[2], POOL_HALF[3])))
    acc = jnp.zeros((tm, POOL_W), F32)
    for k in range(-HALO, HALO):
        xk = ext_ref[HALO + k:HALO + k + tm, 0:256]
        ok = ((t + k >= 0) & (t + k < seqlen)) & ((k >= -half) & (k < half))
        acc = acc + jnp.where(ok, xk, 0.0)
    cnt = (jnp.minimum(t + half, seqlen) - jnp.maximum(t - half, 0)).astype(F32)
    dpool = (acc / cnt - zp).astype(BF16)
    o_ref[:, 0:256] = (_dot(dpool, wpool_ref[...]) * pscale_ref[...]).astype(BF16)

    vc = (_rms(zv) * gsgu_ref[...]).astype(BF16)
    wsgu = wsgu_ref[...]
    bias = bsgu_ref[...]
    for n in range(tm // CHUNK):
        rs = slice(n * CHUNK, (n + 1) * CHUNK)
        r = _dot(wsgu, vc[rs, :])
        mixed = jnp.where(grp == 0, r[0:CHUNK],
                          jnp.where(grp == 1, r[CHUNK:2 * CHUNK],
                                    jnp.where(grp == 2, r[2 * CHUNK:3 * CHUNK], r[3 * CHUNK:])))
        o_ref[rs, 256:512] = (zu[rs, :] * (mixed + bias)).astype(BF16)

    pm1 = jnp.where(t - 1 >= 0, ext_ref[HALO - 1:HALO - 1 + tm, 256:512], 0.0)
    pp1 = jnp.where(t + 1 < seqlen, ext_ref[HALO + 1:HALO + 1 + tm, 256:512], 0.0)
    conv = pm1 * convw_ref[0:1, :] + prod * convw_ref[1:2, :] + pp1 * convw_ref[2:3, :]
    o_ref[:, 512:768] = (zb * conv).astype(BF16)


def _local_call(zloc, seqlen, wpool, pscale, gsgu, wsgu, bsgu, convw):
    rows = zloc.shape[0]
    tm = TM_LOCAL
    hb = tm // HALO
    last = rows // HALO - 1
    return pl.pallas_call(
        functools.partial(_local_kernel, seqlen=seqlen, tm=tm),
        grid=(rows // tm,),
        in_specs=[
            pl.BlockSpec((tm, LOC_W), lambda i: (i, 0)),
            pl.BlockSpec((HALO, LOC_W), lambda i: (jnp.maximum(i * hb - 1, 0), 0)),
            pl.BlockSpec((HALO, LOC_W), lambda i: (jnp.minimum((i + 1) * hb, last), 0)),
            _const_spec(wpool.shape),
            _const_spec(pscale.shape),
            _const_spec(gsgu.shape),
            _const_spec(wsgu.shape),
            _const_spec(bsgu.shape),
            _const_spec(convw.shape),
        ],
        out_specs=pl.BlockSpec((tm, 768), lambda i: (i, 0)),
        out_shape=jax.ShapeDtypeStruct((rows, 768), BF16),
        scratch_shapes=[pltpu.VMEM((tm + 2 * HALO, 512), F32)],
        compiler_params=_params(("parallel",)),
        name="local_mixers",
    )(zloc, zloc, zloc, wpool, pscale, gsgu, wsgu, bsgu, convw)


def _attn_kernel(q_ref, k_ref, v_ref, o_ref):
    lane = lax.broadcasted_iota(jnp.int32, (1, HEAD_PAD), 1)
    for j in range(MLA_HEADS // 2):
        vs = v_ref[:, j * HEAD_PAD:(j + 1) * HEAD_PAD]
        outs = []
        for hd in (2 * j, 2 * j + 1):
            sl = slice(hd * HEAD_PAD, (hd + 1) * HEAD_PAD)
            s = lax.dot_general(q_ref[:, sl], k_ref[:, sl], (((1,), (1,)), ((), ())),
                                preferred_element_type=F32)
            p = jnp.exp(s - jnp.max(s, axis=-1, keepdims=True))
            l = jnp.sum(p, axis=-1, keepdims=True)
            outs.append(_dot(p.astype(BF16), vs) / l)
        o_ref[:, j * HEAD_PAD:(j + 1) * HEAD_PAD] = jnp.where(lane < V_HEAD, outs[0], outs[1]).astype(BF16)


def _attn_call(q, k, v):
    b, lq, _ = q.shape
    lk = k.shape[1]
    tq = TQ_ATTN
    return pl.pallas_call(
        _attn_kernel,
        grid=(b, lq // tq),
        in_specs=[
            pl.BlockSpec((None, tq, QK_PAD), lambda i, j: (i, j, 0)),
            pl.BlockSpec((None, lk, QK_PAD), lambda i, j: (i, 0, 0)),
            pl.BlockSpec((None, lk, MLA_W), lambda i, j: (i, 0, 0)),
        ],
        out_specs=pl.BlockSpec((None, tq, MLA_W), lambda i, j: (i, j, 0)),
        out_shape=jax.ShapeDtypeStruct((b, lq, MLA_W), BF16),
        compiler_params=_params(("parallel", "parallel")),
        name="mla_attention",
    )(q, k, v)


def _merge_kernel(x_ref, mods_ref, gpre_ref, gpost_ref, loc_ref, att_ref, wgate_ref, bgate_ref,
                  wbp_ref, wbs_ref, wbm_ref, wbc_ref, wo_ref, o_ref):
    x = x_ref[...]
    h = _pre(x, gpre_ref[...], mods_ref, 1).astype(BF16)
    branches = (
        (loc_ref[:, 0:256], wbp_ref),
        (loc_ref[:, 256:512], wbs_ref),
        (att_ref[...], wbm_ref),
        (loc_ref[:, 512:768], wbc_ref),
    )
    merged = None
    for j, (inp, w_ref) in enumerate(branches):
        sl = slice(j * D_MODEL, (j + 1) * D_MODEL)
        gate = _sigmoid(_dot(h, wgate_ref[:, sl]) + bgate_ref[:, sl])
        term = gate * _dot(inp, w_ref[...])
        merged = term if merged is None else merged + term
    m = _dot(merged.astype(BF16), wo_ref[...])
    o_ref[...] = x + mods_ref[5:6, :] * (_rms(m) * gpost_ref[...])


def _merge_call(x, mods, gpre, gpost, loc, att, wgate, bgate, wbp, wbs, wbm, wbc, wo):
    rows = x.shape[0]
    tm = TM_TOKEN
    per_cond = rows // mods.shape[0] // tm
    row_spec = lambda w: pl.BlockSpec((tm, w), lambda i: (i, 0))
    return pl.pallas_call(
        _merge_kernel,
        grid=(rows // tm,),
        in_specs=[
            row_spec(D_MODEL),
            pl.BlockSpec((None, N_MOD, D_MODEL), lambda i: (i // per_cond, 0, 0)),
            _const_spec((1, D_MODEL)),
            _const_spec((1, D_MODEL)),
            row_spec(768),
            row_spec(MLA_W),
            _const_spec(wgate.shape),
            _const_spec(bgate.shape),
            _const_spec(wbp.shape),
            _const_spec(wbs.shape),
            _const_spec(wbm.shape),
            _const_spec(wbc.shape),
            _const_spec(wo.shape),
        ],
        out_specs=row_spec(D_MODEL),
        out_shape=jax.ShapeDtypeStruct((rows, D_MODEL), F32),
        compiler_params=_params(("parallel",)),
        name="gated_merge",
    )(x, mods, gpre, gpost, loc, att, wgate, bgate, wbp, wbs, wbm, wbc, wo)


def _rope_swap_perm():
    j = np.arange(QK_ROPE)
    return np.where((j % 16) < 8, j + 8, j - 8)


def _layer_weights(l, w_in, w_uq, w_ukv, w_pool, w_sgu, b_sgu):
    perm = _rope_swap_perm()
    wi = w_in[l]
    zp, zu, zv, cq, ckv, kr, zb, zc, zx = jnp.split(wi, np.cumsum((256, 256, 256, 256, 128, 32, 256, 256))[:], axis=1)
    pad = lambda a, lo, hi: jnp.pad(a, ((0, 0), (lo, hi)))
    kr_placed = pad(kr, QK_NOPE, HEAD_PAD - QK_NOPE - QK_ROPE)
    kr_swapped = pad(kr[:, perm], QK_NOPE, HEAD_PAD - QK_NOPE - QK_ROPE)
    win_ctx = jnp.concatenate([zp, zu, zv, zb, zc, zx, cq, ckv, kr_placed], axis=1).astype(BF16)
    win_lat = jnp.concatenate([zp, zu, zv, zb, zc, zx, cq, ckv, kr_placed, kr_swapped], axis=1).astype(BF16)

    uq = w_uq[l].reshape(Q_LORA, MLA_HEADS, QK_NOPE + QK_ROPE)
    wq = jnp.pad(uq, ((0, 0), (0, 0), (0, HEAD_PAD - QK_NOPE - QK_ROPE))).reshape(Q_LORA, QK_PAD)
    uq_sw = uq[:, :, QK_NOPE:][:, :, perm]
    wq_sw = jnp.pad(uq_sw, ((0, 0), (0, 0), (QK_NOPE, HEAD_PAD - QK_NOPE - QK_ROPE))).reshape(Q_LORA, QK_PAD)
    wq_ctx = wq.astype(BF16)
    wq_lat = jnp.concatenate([wq, wq_sw], axis=1).astype(BF16)

    ukv = w_ukv[l].reshape(KV_LORA, MLA_HEADS, QK_NOPE + V_HEAD)
    wk = jnp.pad(ukv[:, :, :QK_NOPE], ((0, 0), (0, 0), (0, HEAD_PAD - QK_NOPE))).reshape(KV_LORA, QK_PAD).astype(BF16)
    wv = ukv[:, :, QK_NOPE:].reshape(KV_LORA, MLA_W).astype(BF16)

    wpool = jnp.zeros((POOL_W, POOL_W), F32)
    for g in range(POOL_GROUPS):
        wpool = wpool.at[g * POOL_GW:(g + 1) * POOL_GW, g * POOL_GW:(g + 1) * POOL_GW].set(w_pool[l, g])
    wsgu = w_sgu[l].reshape(SGU_GROUPS * CHUNK, CHUNK).astype(BF16)
    bsgu = jnp.repeat(b_sgu[l].T, SGU_W // SGU_GROUPS, axis=1)
    return dict(win_ctx=win_ctx, win_lat=win_lat, wq_ctx=wq_ctx, wq_lat=wq_lat, wk=wk, wv=wv,
                wpool=wpool.astype(BF16), wsgu=wsgu, bsgu=bsgu)


def _rope_tables(n):
    rows = n // GRID_W
    r = jnp.broadcast_to(jnp.arange(rows)[:, None], (rows, GRID_W)).reshape(-1).astype(F32)
    col = jnp.broadcast_to(jnp.arange(GRID_W)[None, :], (rows, GRID_W)).reshape(-1).astype(F32)
    half = QK_ROPE // 2
    freqs = ROPE_THETA ** (-(2.0 * jnp.arange(half // 2, dtype=F32)) / half)
    ang = jnp.stack([r[:, None] * freqs, col[:, None] * freqs], axis=1)
    cos, sin = jnp.cos(ang), jnp.sin(ang)
    cos32 = jnp.concatenate([cos, cos], axis=-1).reshape(n, QK_ROPE)
    sin32 = jnp.concatenate([-sin, sin], axis=-1).reshape(n, QK_ROPE)
    cos_tab = jnp.concatenate([jnp.ones((n, QK_NOPE), F32), cos32,
                               jnp.zeros((n, HEAD_PAD - QK_NOPE - QK_ROPE), F32)], axis=1)
    sin_tab = jnp.pad(sin32, ((0, 0), (QK_NOPE, HEAD_PAD - QK_NOPE - QK_ROPE)))
    return cos_tab, sin_tab


def kernel(x_prompt, x_sample, cache_ckv, cache_krope, c, c_ctx, w_mod, b_mod, g_pre, g_post,
           w_ffn_gu, w_ffn_dn, w_in, w_pool, pool_scale, g_sgu, w_sgu, b_sgu, g_q, w_uq, g_kv,
           w_ukv, conv_w, w_br_pool, w_br_sgu, w_br_mla, w_br_conv, w_gate, b_gate, w_o):
    batch, seq, _ = x_prompt.shape
    dec_batch, dec_seq, _ = x_sample.shape

    cond8 = jnp.concatenate([c_ctx[None, :], c, jnp.zeros((8 - 1 - dec_batch, D_MODEL), F32)], axis=0)
    mods_all = _mods_call(cond8, w_mod, b_mod).reshape(DEPTH, 8, N_MOD, D_MODEL)
    rope_tabs = _rope_tables(dec_seq)
    place = jnp.pad(jnp.eye(QK_ROPE, dtype=F32), ((0, 0), (QK_NOPE, HEAD_PAD - QK_NOPE - QK_ROPE))).astype(BF16)

    xs = {"ctx": x_prompt.reshape(batch * seq, D_MODEL), "lat": x_sample.reshape(dec_batch * dec_seq, D_MODEL)}
    ckv_states, kr_states = [], []
    for l in range(DEPTH):
        lw = _layer_weights(l, w_in, w_uq, w_ukv, w_pool, w_sgu, b_sgu)
        wgu = w_ffn_gu[l].astype(BF16)
        wdn = w_ffn_dn[l].astype(BF16)
        wgate = w_gate[l].astype(BF16)
        bgate = b_gate[l][None, :]
        wbp, wbs = w_br_pool[l].astype(BF16), w_br_sgu[l].astype(BF16)
        wbm, wbc = w_br_mla[l].astype(BF16), w_br_conv[l].astype(BF16)
        wo = w_o[l].astype(BF16)
        gpre = [g_pre[l, s][None, :] for s in range(3)]
        gpost = [g_post[l, s][None, :] for s in range(3)]
        k_ctx, v_ctx = _kvup_call(cache_ckv[:, l], cache_krope[:, l], lw["wk"], lw["wv"], place)

        for stream in ("ctx", "lat"):
            is_ctx = stream == "ctx"
            mods = mods_all[l, 0:1] if is_ctx else mods_all[l, 1:1 + dec_batch]
            nb, ns = (batch, seq) if is_ctx else (dec_batch, dec_seq)
            x = xs[stream]
            x = _ffn_call(x, mods, gpre[0], gpost[0], wgu[0], wdn[0], 0)
            front = _front_call(x, mods, gpre[1], lw["win_ctx" if is_ctx else "win_lat"], g_q[l][None, :],
                                lw["wq_ctx" if is_ctx else "wq_lat"], g_kv[l][None, :], lw["wk"], lw["wv"],
                                None if is_ctx else rope_tabs)
            zloc, q, k, v = front[:4]
            if is_ctx:
                ckv_states.append(front[4].reshape(batch, seq, KV_LORA))
                kr_states.append(front[5].reshape(batch, seq, QK_ROPE))
            loc = _local_call(zloc, ns, lw["wpool"], pool_scale[l][None, :], g_sgu[l][None, :],
                              lw["wsgu"], lw["bsgu"], conv_w[l])
            q = q.reshape(nb, ns, QK_PAD)
            k = k.reshape(nb, ns, QK_PAD)
            v = v.reshape(nb, ns, MLA_W)
            if not is_ctx:
                k = jnp.concatenate([k_ctx, k], axis=1)
                v = jnp.concatenate([v_ctx, v], axis=1)
            att = _attn_call(q, k, v).reshape(nb * ns, MLA_W)
            x = _merge_call(x, mods, gpre[1], gpost[1], loc, att, wgate, bgate, wbp, wbs, wbm, wbc, wo)
            x = _ffn_call(x, mods, gpre[2], gpost[2], wgu[1], wdn[1], 2)
            xs[stream] = x

    y_prompt = xs["ctx"].reshape(batch, seq, D_MODEL)
    y_sample = xs["lat"].reshape(dec_batch, dec_seq, D_MODEL)
    state_ckv = jnp.stack(ckv_states, axis=1)
    state_krope = jnp.stack(kr_states, axis=1)
    return (y_prompt, y_sample, state_ckv, state_krope)
```

```python
import functools

import numpy as np
import jax
import jax.numpy as jnp
from jax import lax
from jax.experimental import pallas as pl
from jax.experimental.pallas import tpu as pltpu

F32 = jnp.float32
BF16 = jnp.bfloat16

D_MODEL = 1024
DEPTH = 2
GRID_W = 64
N_MOD = 9
D_FF = 2816
EPS = 1e-6
POOL_GROUPS = 4
POOL_GW = 64
POOL_W = 256
SGU_GROUPS = 4
SGU_W = 256
CHUNK = 128
MLA_HEADS = 8
QK_NOPE = 64
QK_ROPE = 32
V_HEAD = 64
Q_LORA = 256
KV_LORA = 128
MLA_W = MLA_HEADS * V_HEAD
ROPE_THETA = 10000.0
CONV_W = 256
CONV_K = 3
N_BRANCH = 4
HEAD_PAD = 128
QK_PAD = MLA_HEADS * HEAD_PAD
ROPE_PAD = (QK_NOPE, HEAD_PAD - QK_NOPE - QK_ROPE)
LOC_W = 6 * 256
MAIN_W = LOC_W + Q_LORA + KV_LORA
HALO = 8
POOL_HALF = (1, 2, 4, 8)
LOG2E = 1.4426950408889634

VMEM_LIMIT = 56 * 1024 * 1024

TM_TOKEN = 512
TM_LOCAL = 1024
TQ_ATTN = 256


def _sel_spec(arr, idx=()):
    rest = arr.shape[len(idx):]
    zeros = (0,) * len(rest)
    return pl.BlockSpec((None,) * len(idx) + rest, lambda *_: tuple(idx) + zeros,
                        pipeline_mode=pl.Buffered(1))


def _mods_spec(l, cond0, per_cond):
    return pl.BlockSpec((None, None, N_MOD, D_MODEL), lambda i: (l, cond0 + i // per_cond, 0, 0))


def _params(sem):
    return pltpu.CompilerParams(dimension_semantics=sem, vmem_limit_bytes=VMEM_LIMIT)


def _sigmoid(x):
    return 1.0 / (1.0 + jnp.exp(-x))


def _rms(x):
    return x * lax.rsqrt(jnp.mean(x * x, axis=-1, keepdims=True) + EPS)


def _pre(x, g, mods_ref, s):
    return (_rms(x) * g) * (1.0 + mods_ref[3 * s + 1:3 * s + 2, :]) + mods_ref[3 * s:3 * s + 1, :]


def _dot(a, b):
    return jnp.dot(a, b, preferred_element_type=F32)


def _dot_nt(a, b):
    return lax.dot_general(a, b, (((1,), (1,)), ((), ())), preferred_element_type=F32)


def _mods_kernel(cond_ref, w_ref, b_ref, o_ref):
    c = cond_ref[...]
    a = (c * _sigmoid(c)).astype(BF16)
    o_ref[0] = _dot(a, w_ref[0].astype(BF16)) + b_ref[0]


def _mods_call(cond8, w_mod, b_mod):
    n_tile = 1024
    nt = (N_MOD * D_MODEL) // n_tile
    return pl.pallas_call(
        _mods_kernel,
        grid=(DEPTH, nt),
        in_specs=[
            pl.BlockSpec((8, D_MODEL), lambda l, j: (0, 0)),
            pl.BlockSpec((1, D_MODEL, n_tile), lambda l, j: (l, 0, j)),
            pl.BlockSpec((1, 1, n_tile), lambda l, j: (l, 0, j)),
        ],
        out_specs=pl.BlockSpec((1, 8, n_tile), lambda l, j: (l, 0, j)),
        out_shape=jax.ShapeDtypeStruct((DEPTH, 8, N_MOD * D_MODEL), F32),
        compiler_params=_params(("parallel", "parallel")),
        name="adaln_mods",
    )(cond8, w_mod, b_mod.reshape(DEPTH, 1, N_MOD * D_MODEL))


def _ffn_kernel(x_ref, mods_ref, gpre_ref, gpost_ref, wgu_ref, wdn_ref, o_ref, *, s):
    x = x_ref[...]
    h = _pre(x, gpre_ref[...], mods_ref, s).astype(BF16)
    g = _dot(h, wgu_ref[:, :D_FF])
    u = _dot(h, wgu_ref[:, D_FF:])
    a = ((g * _sigmoid(g)) * u).astype(BF16)
    f = _dot(a, wdn_ref[...])
    gate = mods_ref[3 * s + 2:3 * s + 3, :]
    o_ref[...] = x + (0.5 * gate) * (_rms(f) * gpost_ref[...])


def _ffn_call(x, mods, cond, P, l, s):
    rows = x.shape[0]
    tm = TM_TOKEN
    j = s // 2
    return pl.pallas_call(
        functools.partial(_ffn_kernel, s=s),
        grid=(rows // tm,),
        in_specs=[
            pl.BlockSpec((tm, D_MODEL), lambda i: (i, 0)),
            _mods_spec(l, cond[0], rows // cond[1] // tm),
            _sel_spec(P["g_pre"], (l, s)),
            _sel_spec(P["g_post"], (l, s)),
            _sel_spec(P["w_ffn_gu"], (l, j)),
            _sel_spec(P["w_ffn_dn"], (l, j)),
        ],
        out_specs=pl.BlockSpec((tm, D_MODEL), lambda i: (i, 0)),
        out_shape=jax.ShapeDtypeStruct((rows, D_MODEL), F32),
        compiler_params=_params(("parallel",)),
        name="ffn_half_step",
    )(x, mods, P["g_pre"], P["g_post"], P["w_ffn_gu"], P["w_ffn_dn"])


def _front_kernel(*refs, rope):
    if rope:
        (x_ref, mods_ref, gpre_ref, win_ref, gq_ref, wqt_ref, gkv_ref, wk_ref, wvt_ref,
         cos_ref, sin_ref, cost_ref, sint_ref, zloc_ref, qt_ref, k_ref, vt_ref) = refs
    else:
        (x_ref, mods_ref, gpre_ref, win_ref, gq_ref, wqt_ref, gkv_ref, wk_ref, wvt_ref,
         zloc_ref, qt_ref, k_ref, vt_ref, ckv_ref, kr_ref) = refs
    h = _pre(x_ref[...], gpre_ref[...], mods_ref, 1).astype(BF16)
    z = _dot(h, win_ref[...])
    zloc_ref[...] = z[:, :LOC_W]
    qn = (_rms(z[:, LOC_W:LOC_W + Q_LORA]) * gq_ref[...]).astype(BF16)
    ckv_n = _rms(z[:, LOC_W + Q_LORA:MAIN_W]) * gkv_ref[...]
    kr = z[:, MAIN_W:MAIN_W + HEAD_PAD]
    if rope:
        kr = kr * cos_ref[...] + z[:, MAIN_W + HEAD_PAD:] * sin_ref[...]
    else:
        ckv_ref[...] = ckv_n
        kr_ref[...] = kr[:, QK_NOPE:QK_NOPE + QK_ROPE]
    ckv_b = ckv_n.astype(BF16)
    kn = _dot(ckv_b, wk_ref[...])
    vt = _dot_nt(wvt_ref[...], ckv_b)
    qt = _dot_nt(wqt_ref[0:QK_PAD, :], qn)
    if rope:
        qt_sw = _dot_nt(wqt_ref[QK_PAD:, :], qn)
        cost = cost_ref[...]
        sint = sint_ref[...]
    q_scale = float(QK_NOPE + QK_ROPE) ** -0.5 * LOG2E
    for hd in range(MLA_HEADS):
        sl = slice(hd * HEAD_PAD, (hd + 1) * HEAD_PAD)
        qh = qt[sl, :]
        if rope:
            qh = qh * cost + qt_sw[sl, :] * sint
        qt_ref[hd] = (qh * q_scale).astype(BF16)
        k_ref[hd] = (kn[:, sl] + kr).astype(BF16)
        vt_ref[hd] = vt[hd * V_HEAD:(hd + 1) * V_HEAD, :].astype(BF16)


def _front_call(x, mods, cond, P, l, rope_tabs):
    rows = x.shape[0]
    tm = TM_TOKEN
    rope = rope_tabs is not None
    win = P["win_lat"] if rope else P["win_ctx"]
    wqt = P["wqt_lat"] if rope else P["wqt_ctx"]
    row_spec = lambda w: pl.BlockSpec((tm, w), lambda i: (i, 0))
    in_specs = [
        row_spec(D_MODEL),
        _mods_spec(l, cond[0], rows // cond[1] // tm),
        _sel_spec(P["g_pre"], (l, 1)),
        _sel_spec(win, (l,)),
        _sel_spec(P["g_q"], (l,)),
        _sel_spec(wqt, (l,)),
        _sel_spec(P["g_kv"], (l,)),
        _sel_spec(P["wk"], (l,)),
        _sel_spec(P["wvt"], (l,)),
    ]
    args = [x, mods, P["g_pre"], win, P["g_q"], wqt, P["g_kv"], P["wk"], P["wvt"]]
    out_specs = [
        row_spec(LOC_W),
        pl.BlockSpec((MLA_HEADS, HEAD_PAD, tm), lambda i: (0, 0, i)),
        pl.BlockSpec((MLA_HEADS, tm, HEAD_PAD), lambda i: (0, i, 0)),
        pl.BlockSpec((MLA_HEADS, V_HEAD, tm), lambda i: (0, 0, i)),
    ]
    out_shape = [
        jax.ShapeDtypeStruct((rows, LOC_W), F32),
        jax.ShapeDtypeStruct((MLA_HEADS, HEAD_PAD, rows), BF16),
        jax.ShapeDtypeStruct((MLA_HEADS, rows, HEAD_PAD), BF16),
        jax.ShapeDtypeStruct((MLA_HEADS, V_HEAD, rows), BF16),
    ]
    if rope:
        cos_tab, sin_tab, cos_tab_t, sin_tab_t = rope_tabs
        seq_tiles = cos_tab.shape[0] // tm
        in_specs += [pl.BlockSpec((tm, HEAD_PAD), lambda i: (i % seq_tiles, 0))] * 2
        in_specs += [pl.BlockSpec((HEAD_PAD, tm), lambda i: (0, i % seq_tiles))] * 2
        args += [cos_tab, sin_tab, cos_tab_t, sin_tab_t]
    else:
        out_specs += [row_spec(KV_LORA), row_spec(QK_ROPE)]
        out_shape += [jax.ShapeDtypeStruct((rows, KV_LORA), F32),
                      jax.ShapeDtypeStruct((rows, QK_ROPE), F32)]
    return pl.pallas_call(
        functools.partial(_front_kernel, rope=rope),
        grid=(rows // tm,),
        in_specs=in_specs,
        out_specs=out_specs,
        out_shape=out_shape,
        compiler_params=_params(("parallel",)),
        name="mix_front_rope" if rope else "mix_front",
    )(*args)


def _kvup_kernel(ckv_ref, kr_ref, wk_ref, wvt_ref, place_ref, k_ref, vt_ref):
    ckv_b = ckv_ref[...].astype(BF16)
    kn = _dot(ckv_b, wk_ref[...])
    kr = _dot(kr_ref[...].astype(BF16), place_ref[...])
    vt = _dot_nt(wvt_ref[...], ckv_b)
    for hd in range(MLA_HEADS):
        sl = slice(hd * HEAD_PAD, (hd + 1) * HEAD_PAD)
        k_ref[hd] = (kn[:, sl] + kr).astype(BF16)
        vt_ref[hd] = vt[hd * V_HEAD:(hd + 1) * V_HEAD, :].astype(BF16)


def _kvup_call(cache_ckv, cache_krope, P, l, place):
    b, _, n, _ = cache_ckv.shape
    return pl.pallas_call(
        _kvup_kernel,
        grid=(b,),
        in_specs=[
            pl.BlockSpec((None, None, n, KV_LORA), lambda i: (i, l, 0, 0)),
            pl.BlockSpec((None, None, n, QK_ROPE), lambda i: (i, l, 0, 0)),
            _sel_spec(P["wk"], (l,)),
            _sel_spec(P["wvt"], (l,)),
            _sel_spec(place),
        ],
        out_specs=[pl.BlockSpec((None, MLA_HEADS, n, HEAD_PAD), lambda i: (i, 0, 0, 0)),
                   pl.BlockSpec((None, MLA_HEADS, V_HEAD, n), lambda i: (i, 0, 0, 0))],
        out_shape=[jax.ShapeDtypeStruct((b, MLA_HEADS, n, HEAD_PAD), BF16),
                   jax.ShapeDtypeStruct((b, MLA_HEADS, V_HEAD, n), BF16)],
        compiler_params=_params(("parallel",)),
        name="cache_kv_up",
    )(cache_ckv, cache_krope, P["wk"], P["wvt"], place)


def _local_kernel(z_ref, zprev_ref, znext_ref, wpool_ref, pscale_ref, gsgu_ref, wsgu_ref,
                  bsgu_ref, convw_ref, o_ref, ext_ref, *, seqlen, tm):
    i = pl.program_id(0)
    t = (lax.broadcasted_iota(jnp.int32, (tm, 1), 0) + i * tm) & (seqlen - 1)
    lane = lax.broadcasted_iota(jnp.int32, (1, POOL_W), 1)
    grp = lane // POOL_GW

    zp = z_ref[:, 0:256]
    zu = z_ref[:, 256:512]
    zv = z_ref[:, 512:768]
    zb = z_ref[:, 768:1024]
    prod = z_ref[:, 1024:1280] * z_ref[:, 1280:1536]

    ext_ref[0:HALO, 0:256] = zprev_ref[:, 0:256]
    ext_ref[0:HALO, 256:512] = zprev_ref[:, 1024:1280] * zprev_ref[:, 1280:1536]
    ext_ref[HALO:HALO + tm, 0:256] = zp
    ext_ref[HALO:HALO + tm, 256:512] = prod
    ext_ref[HALO + tm:, 0:256] = znext_ref[:, 0:256]
    ext_ref[HALO + tm:, 256:512] = znext_ref[:, 1024:1280] * znext_ref[:, 1280:1536]

    half = jnp.where(grp == 0, POOL_HALF[0],
                     jnp.where(grp == 1, POOL_HALF[1],
                               jnp.where(grp == 2, POOL_HALF[2], POOL_HALF[3])))
    acc = jnp.zeros((tm, POOL_W), F32)
    for k in range(-HALO, HALO):
        xk = ext_ref[HALO + k:HALO + k + tm, 0:256]
        ok = ((t + k >= 0) & (t + k < seqlen)) & ((k >= -half) & (k < half))
        acc = acc + jnp.where(ok, xk, 0.0)
    cnt = (jnp.minimum(t + half, seqlen) - jnp.maximum(t - half, 0)).astype(F32)
    dpool = (acc / cnt - zp).astype(BF16)
    o_ref[:, 0:256] = (_dot(dpool, wpool_ref[...]) * pscale_ref[...]).astype(BF16)

    vc = (_rms(zv) * gsgu_ref[...]).astype(BF16)
    wsgu = wsgu_ref[...]
    bias = bsgu_ref[...]
    for n in range(tm // CHUNK):
        rs = slice(n * CHUNK, (n + 1) * CHUNK)
        r = _dot(wsgu, vc[rs, :])
        mixed = jnp.where(grp == 0, r[0:CHUNK],
                          jnp.where(grp == 1, r[CHUNK:2 * CHUNK],
                                    jnp.where(grp == 2, r[2 * CHUNK:3 * CHUNK], r[3 * CHUNK:])))
        o_ref[rs, 256:512] = (zu[rs, :] * (mixed + bias)).astype(BF16)

    pm1 = jnp.where(t - 1 >= 0, ext_ref[HALO - 1:HALO - 1 + tm, 256:512], 0.0)
    pp1 = jnp.where(t + 1 < seqlen, ext_ref[HALO + 1:HALO + 1 + tm, 256:512], 0.0)
    conv = pm1 * convw_ref[0:1, :] + prod * convw_ref[1:2, :] + pp1 * convw_ref[2:3, :]
    o_ref[:, 512:768] = (zb * conv).astype(BF16)


def _local_call(zloc, seqlen, P, l):
    rows = zloc.shape[0]
    tm = TM_LOCAL
    hb = tm // HALO
    last = rows // HALO - 1
    names = ("wpool", "pool_scale", "g_sgu", "wsgu", "bsgu", "conv_w")
    return pl.pallas_call(
        functools.partial(_local_kernel, seqlen=seqlen, tm=tm),
        grid=(rows // tm,),
        in_specs=[
            pl.BlockSpec((tm, LOC_W), lambda i: (i, 0)),
            pl.BlockSpec((HALO, LOC_W), lambda i: (jnp.maximum(i * hb - 1, 0), 0)),
            pl.BlockSpec((HALO, LOC_W), lambda i: (jnp.minimum((i + 1) * hb, last), 0)),
        ] + [_sel_spec(P[n], (l,)) for n in names],
        out_specs=pl.BlockSpec((tm, 768), lambda i: (i, 0)),
        out_shape=jax.ShapeDtypeStruct((rows, 768), BF16),
        scratch_shapes=[pltpu.VMEM((tm + 2 * HALO, 512), F32)],
        compiler_params=_params(("parallel",)),
        name="local_mixers",
    )(zloc, zloc, zloc, *[P[n] for n in names])


def _attn_kernel(*refs, n_seg):
    qt_ref, qtn_ref = refs[0:2]
    k_refs = refs[2:2 + n_seg]
    vt_refs = refs[2 + n_seg:2 + 2 * n_seg]
    o_ref, sa_ref, sb_ref, ma_ref, mb_ref, ot_ref = refs[2 + 2 * n_seg:]
    buf_a = (sa_ref, ma_ref)
    buf_b = (sb_ref, mb_ref)
    bounds = [0]
    for k_ref in k_refs:
        bounds.append(bounds[-1] + k_ref.shape[1])

    def scores(q_ref, h, buf):
        s_ref, m_ref = buf
        qt = q_ref[h]
        m = None
        for k_ref, lo, hi in zip(k_refs, bounds[:-1], bounds[1:]):
            s = _dot(k_ref[h], qt)
            s_ref[lo:hi, :] = s
            ms = jnp.max(s, axis=0, keepdims=True)
            m = ms if m is None else jnp.maximum(m, ms)
        m_ref[...] = m

    def values(h, buf):
        s_ref, m_ref = buf
        m = m_ref[...]
        l = None
        ot = None
        for vt_ref, lo, hi in zip(vt_refs, bounds[:-1], bounds[1:]):
            p = jnp.exp2(s_ref[lo:hi, :] - m)
            ls = jnp.sum(p, axis=0, keepdims=True)
            os_ = _dot(vt_ref[h], p.astype(BF16))
            l = ls if l is None else l + ls
            ot = os_ if ot is None else ot + os_
        ot_ref[pl.ds(pl.multiple_of(h * V_HEAD, V_HEAD), V_HEAD), :] = ot / l

    @pl.when(pl.program_id(1) == 0)
    def _():
        scores(qt_ref, 0, buf_a)

    def body(h, carry):
        @pl.when(h % 2 == 0)
        def _():
            scores(qt_ref, h + 1, buf_b)
            values(h, buf_a)

        @pl.when(h % 2 == 1)
        def _():
            scores(qt_ref, h + 1, buf_a)
            values(h, buf_b)
        return carry

    lax.fori_loop(0, MLA_HEADS - 1, body, 0)
    scores(qtn_ref, 0, buf_a)
    values(MLA_HEADS - 1, buf_b)
    o_ref[...] = ot_ref[...].T.astype(BF16)


def _attn_call(qt, segs, batch):
    rows = qt.shape[2]
    lq = rows // batch
    tq = TQ_ATTN
    nq = lq // tq
    n_seg = len(segs)
    k_specs, vt_specs, k_args, vt_args, keys = [], [], [], [], 0
    for k, vt in segs:
        if k.ndim == 4:
            n = k.shape[2]
            k_specs.append(pl.BlockSpec((None, MLA_HEADS, n, HEAD_PAD), lambda i, j: (i, 0, 0, 0),
                                        pipeline_mode=pl.Buffered(1)))
            vt_specs.append(pl.BlockSpec((None, MLA_HEADS, V_HEAD, n), lambda i, j: (i, 0, 0, 0),
                                         pipeline_mode=pl.Buffered(1)))
        else:
            n = k.shape[1] // batch
            k_specs.append(pl.BlockSpec((MLA_HEADS, n, HEAD_PAD), lambda i, j: (0, i, 0),
                                        pipeline_mode=pl.Buffered(1)))
            vt_specs.append(pl.BlockSpec((MLA_HEADS, V_HEAD, n), lambda i, j: (0, 0, i),
                                         pipeline_mode=pl.Buffered(1)))
        k_args.append(k)
        vt_args.append(vt)
        keys += n
    assert MLA_HEADS % 2 == 0
    return pl.pallas_call(
        functools.partial(_attn_kernel, n_seg=n_seg),
        grid=(batch, nq),
        in_specs=[
            pl.BlockSpec((MLA_HEADS, HEAD_PAD, tq), lambda i, j: (0, 0, i * nq + j)),
            pl.BlockSpec((MLA_HEADS, HEAD_PAD, tq), lambda i, j: (0, 0, i * nq + jnp.minimum(j + 1, nq - 1))),
        ] + k_specs + vt_specs,
        out_specs=pl.BlockSpec((tq, MLA_W), lambda i, j: (i * nq + j, 0)),
        out_shape=jax.ShapeDtypeStruct((rows, MLA_W), BF16),
        scratch_shapes=[pltpu.VMEM((keys, tq), F32), pltpu.VMEM((keys, tq), F32),
                        pltpu.VMEM((1, tq), F32), pltpu.VMEM((1, tq), F32),
                        pltpu.VMEM((MLA_W, tq), F32)],
        compiler_params=_params(("arbitrary", "arbitrary")),
        name="mla_attention",
    )(qt, qt, *k_args, *vt_args)


def _merge_kernel(x_ref, mods_ref, gpre_ref, gpost_ref, loc_ref, att_ref, wgate_ref, bgate_ref,
                  wbp_ref, wbs_ref, wbm_ref, wbc_ref, wo_ref, o_ref):
    x = x_ref[...]
    h = _pre(x, gpre_ref[...], mods_ref, 1).astype(BF16)
    branches = (
        (loc_ref[:, 0:256], wbp_ref),
        (loc_ref[:, 256:512], wbs_ref),
        (att_ref[...], wbm_ref),
        (loc_ref[:, 512:768], wbc_ref),
    )
    merged = None
    for j, (inp, w_ref) in enumerate(branches):
        sl = slice(j * D_MODEL, (j + 1) * D_MODEL)
        gate = _sigmoid(_dot(h, wgate_ref[:, sl]) + bgate_ref[:, sl])
        term = gate * _dot(inp, w_ref[...])
        merged = term if merged is None else merged + term
    m = _dot(merged.astype(BF16), wo_ref[...])
    o_ref[...] = x + mods_ref[5:6, :] * (_rms(m) * gpost_ref[...])


def _merge_call(x, mods, cond, P, l, loc, att):
    rows = x.shape[0]
    tm = TM_TOKEN
    row_spec = lambda w: pl.BlockSpec((tm, w), lambda i: (i, 0))
    names = ("w_gate", "b_gate", "w_br_pool", "w_br_sgu", "w_br_mla", "w_br_conv", "w_o")
    return pl.pallas_call(
        _merge_kernel,
        grid=(rows // tm,),
        in_specs=[
            row_spec(D_MODEL),
            _mods_spec(l, cond[0], rows // cond[1] // tm),
            _sel_spec(P["g_pre"], (l, 1)),
            _sel_spec(P["g_post"], (l, 1)),
            row_spec(768),
            row_spec(MLA_W),
        ] + [_sel_spec(P[n], (l,)) for n in names],
        out_specs=row_spec(D_MODEL),
        out_shape=jax.ShapeDtypeStruct((rows, D_MODEL), F32),
        compiler_params=_params(("parallel",)),
        name="gated_merge",
    )(x, mods, P["g_pre"], P["g_post"], loc, att, *[P[n] for n in names])


def _rope_swap_perm():
    j = np.arange(QK_ROPE)
    return np.where((j % 16) < 8, j + 8, j - 8)


def _prepare(w_in, w_uq, w_ukv, w_pool, w_sgu, b_sgu):
    perm = _rope_swap_perm()
    zp, zu, zv, cq, ckv, kr, zb, zc, zx = jnp.split(
        w_in, np.cumsum((256, 256, 256, 256, 128, 32, 256, 256)), axis=2)
    pad_rope = lambda a: jnp.pad(a, ((0, 0), (0, 0), ROPE_PAD))
    main = [zp, zu, zv, zb, zc, zx, cq, ckv, pad_rope(kr)]
    win_ctx = jnp.concatenate(main, axis=2).astype(BF16)
    win_lat = jnp.concatenate(main + [pad_rope(kr[:, :, perm])], axis=2).astype(BF16)

    uq = w_uq.reshape(DEPTH, Q_LORA, MLA_HEADS, QK_NOPE + QK_ROPE)
    wq = jnp.pad(uq, ((0, 0), (0, 0), (0, 0), (0, HEAD_PAD - QK_NOPE - QK_ROPE)))
    wq_sw = jnp.pad(uq[:, :, :, QK_NOPE:][:, :, :, perm], ((0, 0), (0, 0), (0, 0), ROPE_PAD))
    to_t = lambda a: a.reshape(DEPTH, Q_LORA, QK_PAD).transpose(0, 2, 1)
    wqt_ctx = to_t(wq).astype(BF16)
    wqt_lat = jnp.concatenate([to_t(wq), to_t(wq_sw)], axis=1).astype(BF16)

    ukv = w_ukv.reshape(DEPTH, KV_LORA, MLA_HEADS, QK_NOPE + V_HEAD)
    wk = jnp.pad(ukv[:, :, :, :QK_NOPE], ((0, 0), (0, 0), (0, 0), (0, HEAD_PAD - QK_NOPE)))
    wk = wk.reshape(DEPTH, KV_LORA, QK_PAD).astype(BF16)
    wvt = ukv[:, :, :, QK_NOPE:].reshape(DEPTH, KV_LORA, MLA_W).transpose(0, 2, 1).astype(BF16)

    eye = jnp.eye(POOL_GROUPS, dtype=F32)
    wpool = jnp.einsum("lgcd,gh->lgchd", w_pool, eye).reshape(DEPTH, POOL_W, POOL_W).astype(BF16)
    wsgu = w_sgu.reshape(DEPTH, SGU_GROUPS * CHUNK, CHUNK).astype(BF16)
    bsgu = jnp.repeat(b_sgu.transpose(0, 2, 1), SGU_W // SGU_GROUPS, axis=2)
    return dict(win_ctx=win_ctx, win_lat=win_lat, wqt_ctx=wqt_ctx, wqt_lat=wqt_lat, wk=wk, wvt=wvt,
                wpool=wpool, wsgu=wsgu, bsgu=bsgu)


def _rope_tables(n):
    rows = n // GRID_W
    r = jnp.broadcast_to(jnp.arange(rows)[:, None], (rows, GRID_W)).reshape(-1).astype(F32)
    col = jnp.broadcast_to(jnp.arange(GRID_W)[None, :], (rows, GRID_W)).reshape(-1).astype(F32)
    half = QK_ROPE // 2
    freqs = ROPE_THETA ** (-(2.0 * jnp.arange(half // 2, dtype=F32)) / half)
    ang = jnp.stack([r[:, None] * freqs, col[:, None] * freqs], axis=1)
    cos, sin = jnp.cos(ang), jnp.sin(ang)
    cos32 = jnp.concatenate([cos, cos], axis=-1).reshape(n, QK_ROPE)
    sin32 = jnp.concatenate([-sin, sin], axis=-1).reshape(n, QK_ROPE)
    cos_tab = jnp.concatenate([jnp.ones((n, QK_NOPE), F32), cos32,
                               jnp.zeros((n, HEAD_PAD - QK_NOPE - QK_ROPE), F32)], axis=1)
    sin_tab = jnp.pad(sin32, ((0, 0), ROPE_PAD))
    return cos_tab, sin_tab, cos_tab.T, sin_tab.T


def kernel(x_prompt, x_sample, cache_ckv, cache_krope, c, c_ctx, w_mod, b_mod, g_pre, g_post,
           w_ffn_gu, w_ffn_dn, w_in, w_pool, pool_scale, g_sgu, w_sgu, b_sgu, g_q, w_uq, g_kv,
           w_ukv, conv_w, w_br_pool, w_br_sgu, w_br_mla, w_br_conv, w_gate, b_gate, w_o):
    batch, seq, _ = x_prompt.shape
    dec_batch, dec_seq, _ = x_sample.shape

    cond8 = jnp.concatenate([c_ctx[None, :], c, jnp.zeros((8 - 1 - dec_batch, D_MODEL), F32)], axis=0)
    mods = _mods_call(cond8, w_mod, b_mod).reshape(DEPTH, 8, N_MOD, D_MODEL)
    rope_tabs = _rope_tables(dec_seq)
    place = jnp.pad(jnp.eye(QK_ROPE, dtype=F32), ((0, 0), ROPE_PAD)).astype(BF16)

    P = _prepare(w_in, w_uq, w_ukv, w_pool, w_sgu, b_sgu)
    row = lambda a: a.reshape(a.shape[:-1] + (1, a.shape[-1]))
    P.update(
        g_pre=row(g_pre), g_post=row(g_post), g_q=row(g_q), g_kv=row(g_kv), g_sgu=row(g_sgu),
        pool_scale=row(pool_scale), b_gate=row(b_gate), conv_w=conv_w,
        w_ffn_gu=w_ffn_gu.astype(BF16), w_ffn_dn=w_ffn_dn.astype(BF16), w_gate=w_gate.astype(BF16),
        w_br_pool=w_br_pool.astype(BF16), w_br_sgu=w_br_sgu.astype(BF16),
        w_br_mla=w_br_mla.astype(BF16), w_br_conv=w_br_conv.astype(BF16), w_o=w_o.astype(BF16),
    )

    streams = {"ctx": (x_prompt.reshape(batch * seq, D_MODEL), (0, 1), batch, seq),
               "lat": (x_sample.reshape(dec_batch * dec_seq, D_MODEL), (1, dec_batch), dec_batch, dec_seq)}
    outs = {}
    ckv_states, kr_states = [], []
    for name, (x, cond, nb, ns) in streams.items():
        is_ctx = name == "ctx"
        for l in range(DEPTH):
            x = _ffn_call(x, mods, cond, P, l, 0)
            front = _front_call(x, mods, cond, P, l, None if is_ctx else rope_tabs)
            zloc, qt, k, vt = front[:4]
            segs = [(k, vt)]
            if is_ctx:
                ckv_states.append(front[4].reshape(batch, seq, KV_LORA))
                kr_states.append(front[5].reshape(batch, seq, QK_ROPE))
            else:
                segs.insert(0, _kvup_call(cache_ckv, cache_krope, P, l, place))
            loc = _local_call(zloc, ns, P, l)
            att = _attn_call(qt, segs, nb)
            x = _merge_call(x, mods, cond, P, l, loc, att)
            x = _ffn_call(x, mods, cond, P, l, 2)
        outs[name] = x

    y_prompt = outs["ctx"].reshape(batch, seq, D_MODEL)
    y_sample = outs["lat"].reshape(dec_batch, dec_seq, D_MODEL)
    state_ckv = jnp.stack(ckv_states, axis=1)
    state_krope = jnp.stack(kr_states, axis=1)
    return (y_prompt, y_sample, state_ckv, state_krope)
```

```python
import functools

import numpy as np
import jax
import jax.numpy as jnp
from jax import lax
from jax.experimental import pallas as pl
from jax.experimental.pallas import tpu as pltpu

F32 = jnp.float32
BF16 = jnp.bfloat16

D_MODEL = 1024
DEPTH = 2
GRID_W = 64
N_MOD = 9
D_FF = 2816
EPS = 1e-6
POOL_GROUPS = 4
POOL_GW = 64
POOL_W = 256
SGU_GROUPS = 4
SGU_W = 256
CHUNK = 128
MLA_HEADS = 8
QK_NOPE = 64
QK_ROPE = 32
V_HEAD = 64
Q_LORA = 256
KV_LORA = 128
MLA_W = MLA_HEADS * V_HEAD
ROPE_THETA = 10000.0
CONV_W = 256
CONV_K = 3
N_BRANCH = 4
HEAD_PAD = 128
QK_PAD = MLA_HEADS * HEAD_PAD
ROPE_PAD = (QK_NOPE, HEAD_PAD - QK_NOPE - QK_ROPE)
LOC_W = 6 * 256
MAIN_W = LOC_W + Q_LORA + KV_LORA
HALO = 8
POOL_HALF = (1, 2, 4, 8)
LOG2E = 1.4426950408889634

VMEM_LIMIT = 56 * 1024 * 1024

TM_TOKEN = 512
TM_LOCAL = 1024
TQ_ATTN = 256


def _sel_spec(arr, idx=()):
    rest = arr.shape[len(idx):]
    zeros = (0,) * len(rest)
    return pl.BlockSpec((None,) * len(idx) + rest, lambda *_: tuple(idx) + zeros,
                        pipeline_mode=pl.Buffered(1))


def _mods_spec(l, cond0, per_cond):
    return pl.BlockSpec((None, None, N_MOD, D_MODEL), lambda i: (l, cond0 + i // per_cond, 0, 0))


def _params(sem):
    return pltpu.CompilerParams(dimension_semantics=sem, vmem_limit_bytes=VMEM_LIMIT)


def _sigmoid(x):
    return 1.0 / (1.0 + jnp.exp(-x))


def _rms(x):
    return x * lax.rsqrt(jnp.mean(x * x, axis=-1, keepdims=True) + EPS)


def _pre(x, g, mods_ref, s):
    return (_rms(x) * g) * (1.0 + mods_ref[3 * s + 1:3 * s + 2, :]) + mods_ref[3 * s:3 * s + 1, :]


def _dot(a, b):
    return jnp.dot(a, b, preferred_element_type=F32)


def _dot_nt(a, b):
    return lax.dot_general(a, b, (((1,), (1,)), ((), ())), preferred_element_type=F32)


def _mods_kernel(cond_ref, w_ref, b_ref, o_ref):
    c = cond_ref[...]
    a = (c * _sigmoid(c)).astype(BF16)
    o_ref[0] = _dot(a, w_ref[0].astype(BF16)) + b_ref[0]


def _mods_call(cond8, w_mod, b_mod):
    n_tile = 1024
    nt = (N_MOD * D_MODEL) // n_tile
    return pl.pallas_call(
        _mods_kernel,
        grid=(DEPTH, nt),
        in_specs=[
            pl.BlockSpec((8, D_MODEL), lambda l, j: (0, 0)),
            pl.BlockSpec((1, D_MODEL, n_tile), lambda l, j: (l, 0, j)),
            pl.BlockSpec((1, 1, n_tile), lambda l, j: (l, 0, j)),
        ],
        out_specs=pl.BlockSpec((1, 8, n_tile), lambda l, j: (l, 0, j)),
        out_shape=jax.ShapeDtypeStruct((DEPTH, 8, N_MOD * D_MODEL), F32),
        compiler_params=_params(("parallel", "parallel")),
        name="adaln_mods",
    )(cond8, w_mod, b_mod.reshape(DEPTH, 1, N_MOD * D_MODEL))


def _ffn_kernel(x_ref, mods_ref, gpre_ref, gpost_ref, wgu_ref, wdn_ref, o_ref, *, s):
    x = x_ref[...]
    h = _pre(x, gpre_ref[...], mods_ref, s).astype(BF16)
    g = _dot(h, wgu_ref[:, :D_FF])
    u = _dot(h, wgu_ref[:, D_FF:])
    a = ((g * _sigmoid(g)) * u).astype(BF16)
    f = _dot(a, wdn_ref[...])
    gate = mods_ref[3 * s + 2:3 * s + 3, :]
    o_ref[...] = x + (0.5 * gate) * (_rms(f) * gpost_ref[...])


def _ffn_call(x, mods, cond, P, l, s):
    rows = x.shape[0]
    tm = TM_TOKEN
    j = s // 2
    return pl.pallas_call(
        functools.partial(_ffn_kernel, s=s),
        grid=(rows // tm,),
        in_specs=[
            pl.BlockSpec((tm, D_MODEL), lambda i: (i, 0)),
            _mods_spec(l, cond[0], rows // cond[1] // tm),
            _sel_spec(P["g_pre"], (l, s)),
            _sel_spec(P["g_post"], (l, s)),
            _sel_spec(P["w_ffn_gu"], (l, j)),
            _sel_spec(P["w_ffn_dn"], (l, j)),
        ],
        out_specs=pl.BlockSpec((tm, D_MODEL), lambda i: (i, 0)),
        out_shape=jax.ShapeDtypeStruct((rows, D_MODEL), F32),
        compiler_params=_params(("parallel",)),
        name="ffn_half_step",
    )(x, mods, P["g_pre"], P["g_post"], P["w_ffn_gu"], P["w_ffn_dn"])


def _front_kernel(*refs, latent):
    if latent:
        (x_ref, mods_ref, gpre_ref, win_ref, gq_ref, wq_ref, gkv_ref, wk_ref, wv_ref,
         cos_ref, sin_ref, cost_ref, sint_ref, zloc_ref, q_ref, k_ref, v_ref) = refs
    else:
        (x_ref, mods_ref, gpre_ref, win_ref, gq_ref, wq_ref, gkv_ref, wk_ref, wv_ref,
         zloc_ref, q_ref, k_ref, v_ref, ckv_ref, kr_ref) = refs
    h = _pre(x_ref[...], gpre_ref[...], mods_ref, 1).astype(BF16)
    z = _dot(h, win_ref[...])
    zloc_ref[...] = z[:, :LOC_W]
    qn = (_rms(z[:, LOC_W:LOC_W + Q_LORA]) * gq_ref[...]).astype(BF16)
    ckv_n = _rms(z[:, LOC_W + Q_LORA:MAIN_W]) * gkv_ref[...]
    kr = z[:, MAIN_W:MAIN_W + HEAD_PAD]
    ckv_b = ckv_n.astype(BF16)
    kn = _dot(ckv_b, wk_ref[...])
    q_scale = float(QK_NOPE + QK_ROPE) ** -0.5 * LOG2E
    if latent:
        kr = kr * cos_ref[...] + z[:, MAIN_W + HEAD_PAD:] * sin_ref[...]
        vt = _dot_nt(wv_ref[...], ckv_b)
        qt = _dot_nt(wq_ref[0:QK_PAD, :], qn)
        qt_sw = _dot_nt(wq_ref[QK_PAD:, :], qn)
        cost = cost_ref[...]
        sint = sint_ref[...]
        for hd in range(MLA_HEADS):
            sl = slice(hd * HEAD_PAD, (hd + 1) * HEAD_PAD)
            q_ref[hd] = ((qt[sl, :] * cost + qt_sw[sl, :] * sint) * q_scale).astype(BF16)
            k_ref[hd] = (kn[:, sl] + kr).astype(BF16)
            v_ref[hd] = vt[hd * V_HEAD:(hd + 1) * V_HEAD, :].astype(BF16)
    else:
        ckv_ref[...] = ckv_n
        kr_ref[...] = kr[:, QK_NOPE:QK_NOPE + QK_ROPE]
        v_ref[...] = _dot(ckv_b, wv_ref[...]).astype(BF16)
        q_ref[...] = (_dot(qn, wq_ref[...]) * q_scale).astype(BF16)
        for hd in range(MLA_HEADS):
            sl = slice(hd * HEAD_PAD, (hd + 1) * HEAD_PAD)
            k_ref[:, sl] = (kn[:, sl] + kr).astype(BF16)


def _front_call(x, mods, cond, P, l, rope_tabs):
    rows = x.shape[0]
    tm = TM_TOKEN
    latent = rope_tabs is not None
    win = P["win_lat"] if latent else P["win_ctx"]
    wq = P["wqt_lat"] if latent else P["wq_ctx"]
    wv = P["wvt"] if latent else P["wv"]
    row_spec = lambda w: pl.BlockSpec((tm, w), lambda i: (i, 0))
    in_specs = [
        row_spec(D_MODEL),
        _mods_spec(l, cond[0], rows // cond[1] // tm),
        _sel_spec(P["g_pre"], (l, 1)),
        _sel_spec(win, (l,)),
        _sel_spec(P["g_q"], (l,)),
        _sel_spec(wq, (l,)),
        _sel_spec(P["g_kv"], (l,)),
        _sel_spec(P["wk"], (l,)),
        _sel_spec(wv, (l,)),
    ]
    args = [x, mods, P["g_pre"], win, P["g_q"], wq, P["g_kv"], P["wk"], wv]
    if latent:
        cos_tab, sin_tab, cos_tab_t, sin_tab_t = rope_tabs
        seq_tiles = cos_tab.shape[0] // tm
        in_specs += [pl.BlockSpec((tm, HEAD_PAD), lambda i: (i % seq_tiles, 0))] * 2
        in_specs += [pl.BlockSpec((HEAD_PAD, tm), lambda i: (0, i % seq_tiles))] * 2
        args += [cos_tab, sin_tab, cos_tab_t, sin_tab_t]
        out_specs = [
            row_spec(LOC_W),
            pl.BlockSpec((MLA_HEADS, HEAD_PAD, tm), lambda i: (0, 0, i)),
            pl.BlockSpec((MLA_HEADS, tm, HEAD_PAD), lambda i: (0, i, 0)),
            pl.BlockSpec((MLA_HEADS, V_HEAD, tm), lambda i: (0, 0, i)),
        ]
        out_shape = [
            jax.ShapeDtypeStruct((rows, LOC_W), F32),
            jax.ShapeDtypeStruct((MLA_HEADS, HEAD_PAD, rows), BF16),
            jax.ShapeDtypeStruct((MLA_HEADS, rows, HEAD_PAD), BF16),
            jax.ShapeDtypeStruct((MLA_HEADS, V_HEAD, rows), BF16),
        ]
    else:
        out_specs = [row_spec(LOC_W), row_spec(QK_PAD), row_spec(QK_PAD), row_spec(MLA_W),
                     row_spec(KV_LORA), row_spec(QK_ROPE)]
        out_shape = [
            jax.ShapeDtypeStruct((rows, LOC_W), F32),
            jax.ShapeDtypeStruct((rows, QK_PAD), BF16),
            jax.ShapeDtypeStruct((rows, QK_PAD), BF16),
            jax.ShapeDtypeStruct((rows, MLA_W), BF16),
            jax.ShapeDtypeStruct((rows, KV_LORA), F32),
            jax.ShapeDtypeStruct((rows, QK_ROPE), F32),
        ]
    return pl.pallas_call(
        functools.partial(_front_kernel, latent=latent),
        grid=(rows // tm,),
        in_specs=in_specs,
        out_specs=out_specs,
        out_shape=out_shape,
        compiler_params=_params(("parallel",)),
        name="mix_front_latent" if latent else "mix_front_context",
    )(*args)


def _kvup_kernel(ckv_ref, kr_ref, wk_ref, wvt_ref, place_ref, k_ref, vt_ref):
    ckv_b = ckv_ref[...].astype(BF16)
    kn = _dot(ckv_b, wk_ref[...])
    kr = _dot(kr_ref[...].astype(BF16), place_ref[...])
    vt = _dot_nt(wvt_ref[...], ckv_b)
    for hd in range(MLA_HEADS):
        sl = slice(hd * HEAD_PAD, (hd + 1) * HEAD_PAD)
        k_ref[hd] = (kn[:, sl] + kr).astype(BF16)
        vt_ref[hd] = vt[hd * V_HEAD:(hd + 1) * V_HEAD, :].astype(BF16)


def _kvup_call(cache_ckv, cache_krope, P, l, place):
    b, _, n, _ = cache_ckv.shape
    return pl.pallas_call(
        _kvup_kernel,
        grid=(b,),
        in_specs=[
            pl.BlockSpec((None, None, n, KV_LORA), lambda i: (i, l, 0, 0)),
            pl.BlockSpec((None, None, n, QK_ROPE), lambda i: (i, l, 0, 0)),
            _sel_spec(P["wk"], (l,)),
            _sel_spec(P["wvt"], (l,)),
            _sel_spec(place),
        ],
        out_specs=[pl.BlockSpec((None, MLA_HEADS, n, HEAD_PAD), lambda i: (i, 0, 0, 0)),
                   pl.BlockSpec((None, MLA_HEADS, V_HEAD, n), lambda i: (i, 0, 0, 0))],
        out_shape=[jax.ShapeDtypeStruct((b, MLA_HEADS, n, HEAD_PAD), BF16),
                   jax.ShapeDtypeStruct((b, MLA_HEADS, V_HEAD, n), BF16)],
        compiler_params=_params(("parallel",)),
        name="cache_kv_up",
    )(cache_ckv, cache_krope, P["wk"], P["wvt"], place)


def _by_group(grp, r):
    return jnp.where(grp == 0, r[0:CHUNK],
                     jnp.where(grp == 1, r[CHUNK:2 * CHUNK],
                               jnp.where(grp == 2, r[2 * CHUNK:3 * CHUNK], r[3 * CHUNK:])))


def _local_kernel(z_ref, zprev_ref, znext_ref, band_ref, wpool_ref, pscale_ref, gsgu_ref, wsgu_ref,
                  bsgu_ref, convw_ref, o_ref, ext_ref, *, seqlen, tm):
    i = pl.program_id(0)
    starts_seq = (i * tm) % seqlen == 0
    ends_seq = ((i + 1) * tm) % seqlen == 0
    t = (lax.broadcasted_iota(jnp.int32, (tm, 1), 0) + i * tm) & (seqlen - 1)
    lane = lax.broadcasted_iota(jnp.int32, (1, POOL_W), 1)
    grp = lane // POOL_GW

    zp = z_ref[:, 0:256]
    zu = z_ref[:, 256:512]
    zv = z_ref[:, 512:768]
    zb = z_ref[:, 768:1024]
    prod = z_ref[:, 1024:1280] * z_ref[:, 1280:1536]

    ext_ref[0:HALO, 0:256] = jnp.where(starts_seq, 0.0, zprev_ref[:, 0:256])
    ext_ref[0:HALO, 256:512] = jnp.where(starts_seq, 0.0, zprev_ref[:, 1024:1280] * zprev_ref[:, 1280:1536])
    ext_ref[HALO:HALO + tm, 0:256] = zp
    ext_ref[HALO:HALO + tm, 256:512] = prod
    ext_ref[HALO + tm:2 * HALO + tm, 0:256] = jnp.where(ends_seq, 0.0, znext_ref[:, 0:256])
    ext_ref[HALO + tm:2 * HALO + tm, 256:512] = jnp.where(
        ends_seq, 0.0, znext_ref[:, 1024:1280] * znext_ref[:, 1280:1536])
    ext_ref[2 * HALO + tm:, :] = jnp.zeros((CHUNK - 2 * HALO, 512), F32)

    xe = ext_ref[:, 0:256]
    hi = xe.astype(BF16)
    lo = (xe - hi.astype(F32)).astype(BF16)
    band = band_ref[...]
    sums = []
    for n in range(tm // CHUNK):
        ws = slice(n * CHUNK, n * CHUNK + 2 * CHUNK)
        sums.append(_by_group(grp, _dot(band, hi[ws, :]) + _dot(band, lo[ws, :])))
    wsum = jnp.concatenate(sums, axis=0) if len(sums) > 1 else sums[0]
    half = jnp.where(grp == 0, POOL_HALF[0],
                     jnp.where(grp == 1, POOL_HALF[1],
                               jnp.where(grp == 2, POOL_HALF[2], POOL_HALF[3])))
    cnt = (jnp.minimum(t + half, seqlen) - jnp.maximum(t - half, 0)).astype(F32)
    dpool = (wsum / cnt - zp).astype(BF16)
    o_ref[:, 0:256] = (_dot(dpool, wpool_ref[...]) * pscale_ref[...]).astype(BF16)

    vc = (_rms(zv) * gsgu_ref[...]).astype(BF16)
    wsgu = wsgu_ref[...]
    bias = bsgu_ref[...]
    for n in range(tm // CHUNK):
        rs = slice(n * CHUNK, (n + 1) * CHUNK)
        mixed = _by_group(grp, _dot(wsgu, vc[rs, :]))
        o_ref[rs, 256:512] = (zu[rs, :] * (mixed + bias)).astype(BF16)

    pm1 = ext_ref[HALO - 1:HALO - 1 + tm, 256:512]
    pp1 = ext_ref[HALO + 1:HALO + 1 + tm, 256:512]
    conv = pm1 * convw_ref[0:1, :] + prod * convw_ref[1:2, :] + pp1 * convw_ref[2:3, :]
    o_ref[:, 512:768] = (zb * conv).astype(BF16)


def _pool_band():
    p = np.arange(CHUNK)[:, None]
    j = np.arange(2 * CHUNK)[None, :]
    bands = [((j >= p + HALO - hw) & (j < p + HALO + hw)) for hw in POOL_HALF]
    return jnp.asarray(np.concatenate(bands, axis=0).astype(np.float32), dtype=BF16)


def _local_call(zloc, seqlen, P, l):
    rows = zloc.shape[0]
    tm = min(TM_LOCAL, seqlen)
    assert seqlen % tm == 0 and tm % CHUNK == 0
    hb = tm // HALO
    last = rows // HALO - 1
    names = ("wpool", "pool_scale", "g_sgu", "wsgu", "bsgu", "conv_w")
    return pl.pallas_call(
        functools.partial(_local_kernel, seqlen=seqlen, tm=tm),
        grid=(rows // tm,),
        in_specs=[
            pl.BlockSpec((tm, LOC_W), lambda i: (i, 0)),
            pl.BlockSpec((HALO, LOC_W), lambda i: (jnp.maximum(i * hb - 1, 0), 0)),
            pl.BlockSpec((HALO, LOC_W), lambda i: (jnp.minimum((i + 1) * hb, last), 0)),
            _sel_spec(P["band"]),
        ] + [_sel_spec(P[n], (l,)) for n in names],
        out_specs=pl.BlockSpec((tm, 768), lambda i: (i, 0)),
        out_shape=jax.ShapeDtypeStruct((rows, 768), BF16),
        scratch_shapes=[pltpu.VMEM((tm + CHUNK, 512), F32)],
        compiler_params=_params(("parallel",)),
        name="local_mixers",
    )(zloc, zloc, zloc, P["band"], *[P[n] for n in names])


def _attn_kernel(*refs, n_seg):
    qt_ref, qtn_ref = refs[0:2]
    k_refs = refs[2:2 + n_seg]
    vt_refs = refs[2 + n_seg:2 + 2 * n_seg]
    o_ref, sa_ref, sb_ref, ma_ref, mb_ref, ot_ref = refs[2 + 2 * n_seg:]
    buf_a = (sa_ref, ma_ref)
    buf_b = (sb_ref, mb_ref)
    bounds = [0]
    for k_ref in k_refs:
        bounds.append(bounds[-1] + k_ref.shape[1])

    def scores(q_ref, h, buf):
        s_ref, m_ref = buf
        qt = q_ref[h]
        m = None
        for k_ref, lo, hi in zip(k_refs, bounds[:-1], bounds[1:]):
            s = _dot(k_ref[h], qt)
            s_ref[lo:hi, :] = s
            ms = jnp.max(s, axis=0, keepdims=True)
            m = ms if m is None else jnp.maximum(m, ms)
        m_ref[...] = m

    def values(h, buf):
        s_ref, m_ref = buf
        m = m_ref[...]
        l = None
        ot = None
        for vt_ref, lo, hi in zip(vt_refs, bounds[:-1], bounds[1:]):
            p = jnp.exp2(s_ref[lo:hi, :] - m)
            ls = jnp.sum(p, axis=0, keepdims=True)
            os_ = _dot(vt_ref[h], p.astype(BF16))
            l = ls if l is None else l + ls
            ot = os_ if ot is None else ot + os_
        ot_ref[pl.ds(pl.multiple_of(h * V_HEAD, V_HEAD), V_HEAD), :] = ot / l

    @pl.when(pl.program_id(1) == 0)
    def _():
        scores(qt_ref, 0, buf_a)

    def body(h, carry):
        @pl.when(h % 2 == 0)
        def _():
            scores(qt_ref, h + 1, buf_b)
            values(h, buf_a)

        @pl.when(h % 2 == 1)
        def _():
            scores(qt_ref, h + 1, buf_a)
            values(h, buf_b)
        return carry

    lax.fori_loop(0, MLA_HEADS - 1, body, 0)
    scores(qtn_ref, 0, buf_a)
    values(MLA_HEADS - 1, buf_b)
    o_ref[...] = ot_ref[...].T.astype(BF16)


def _attn_call(qt, segs, batch):
    rows = qt.shape[2]
    lq = rows // batch
    tq = TQ_ATTN
    nq = lq // tq
    n_seg = len(segs)
    k_specs, vt_specs, k_args, vt_args, keys = [], [], [], [], 0
    for k, vt in segs:
        if k.ndim == 4:
            n = k.shape[2]
            k_specs.append(pl.BlockSpec((None, MLA_HEADS, n, HEAD_PAD), lambda i, j: (i, 0, 0, 0)))
            vt_specs.append(pl.BlockSpec((None, MLA_HEADS, V_HEAD, n), lambda i, j: (i, 0, 0, 0)))
        else:
            n = k.shape[1] // batch
            k_specs.append(pl.BlockSpec((MLA_HEADS, n, HEAD_PAD), lambda i, j: (0, i, 0)))
            vt_specs.append(pl.BlockSpec((MLA_HEADS, V_HEAD, n), lambda i, j: (0, 0, i)))
        k_args.append(k)
        vt_args.append(vt)
        keys += n
    assert MLA_HEADS % 2 == 0
    return pl.pallas_call(
        functools.partial(_attn_kernel, n_seg=n_seg),
        grid=(batch, nq),
        in_specs=[
            pl.BlockSpec((MLA_HEADS, HEAD_PAD, tq), lambda i, j: (0, 0, i * nq + j)),
            pl.BlockSpec((MLA_HEADS, HEAD_PAD, tq), lambda i, j: (0, 0, i * nq + jnp.minimum(j + 1, nq - 1))),
        ] + k_specs + vt_specs,
        out_specs=pl.BlockSpec((tq, MLA_W), lambda i, j: (i * nq + j, 0)),
        out_shape=jax.ShapeDtypeStruct((rows, MLA_W), BF16),
        scratch_shapes=[pltpu.VMEM((keys, tq), F32), pltpu.VMEM((keys, tq), F32),
                        pltpu.VMEM((1, tq), F32), pltpu.VMEM((1, tq), F32),
                        pltpu.VMEM((MLA_W, tq), F32)],
        compiler_params=_params(("arbitrary", "arbitrary")),
        name="mla_attention",
    )(qt, qt, *k_args, *vt_args)


def _attn_short_kernel(q_ref, k_ref, v_ref, o_ref):
    lane = lax.broadcasted_iota(jnp.int32, (1, HEAD_PAD), 1)
    for j in range(MLA_HEADS // 2):
        vs = v_ref[:, j * HEAD_PAD:(j + 1) * HEAD_PAD]
        outs = []
        for hd in (2 * j, 2 * j + 1):
            sl = slice(hd * HEAD_PAD, (hd + 1) * HEAD_PAD)
            s = _dot_nt(q_ref[:, sl], k_ref[:, sl])
            p = jnp.exp2(s - jnp.max(s, axis=-1, keepdims=True))
            l = jnp.sum(p, axis=-1, keepdims=True)
            outs.append(_dot(p.astype(BF16), vs) / l)
        o_ref[:, j * HEAD_PAD:(j + 1) * HEAD_PAD] = jnp.where(lane < V_HEAD, outs[0], outs[1]).astype(BF16)


def _attn_short_call(q, k, v, seq):
    rows = q.shape[0]
    return pl.pallas_call(
        _attn_short_kernel,
        grid=(rows // seq,),
        in_specs=[
            pl.BlockSpec((seq, QK_PAD), lambda i: (i, 0)),
            pl.BlockSpec((seq, QK_PAD), lambda i: (i, 0)),
            pl.BlockSpec((seq, MLA_W), lambda i: (i, 0)),
        ],
        out_specs=pl.BlockSpec((seq, MLA_W), lambda i: (i, 0)),
        out_shape=jax.ShapeDtypeStruct((rows, MLA_W), BF16),
        compiler_params=_params(("parallel",)),
        name="mla_attention_context",
    )(q, k, v)


def _merge_kernel(x_ref, mods_ref, gpre_ref, gpost_ref, loc_ref, att_ref, wgate_ref, bgate_ref,
                  wbp_ref, wbs_ref, wbm_ref, wbc_ref, wo_ref, o_ref):
    x = x_ref[...]
    h = _pre(x, gpre_ref[...], mods_ref, 1).astype(BF16)
    branches = (
        (loc_ref[:, 0:256], wbp_ref),
        (loc_ref[:, 256:512], wbs_ref),
        (att_ref[...], wbm_ref),
        (loc_ref[:, 512:768], wbc_ref),
    )
    merged = None
    for j, (inp, w_ref) in enumerate(branches):
        sl = slice(j * D_MODEL, (j + 1) * D_MODEL)
        gate = _sigmoid(_dot(h, wgate_ref[:, sl]) + bgate_ref[:, sl])
        term = gate * _dot(inp, w_ref[...])
        merged = term if merged is None else merged + term
    m = _dot(merged.astype(BF16), wo_ref[...])
    o_ref[...] = x + mods_ref[5:6, :] * (_rms(m) * gpost_ref[...])


def _merge_call(x, mods, cond, P, l, loc, att):
    rows = x.shape[0]
    tm = TM_TOKEN
    row_spec = lambda w: pl.BlockSpec((tm, w), lambda i: (i, 0))
    names = ("w_gate", "b_gate", "w_br_pool", "w_br_sgu", "w_br_mla", "w_br_conv", "w_o")
    return pl.pallas_call(
        _merge_kernel,
        grid=(rows // tm,),
        in_specs=[
            row_spec(D_MODEL),
            _mods_spec(l, cond[0], rows // cond[1] // tm),
            _sel_spec(P["g_pre"], (l, 1)),
            _sel_spec(P["g_post"], (l, 1)),
            row_spec(768),
            row_spec(MLA_W),
        ] + [_sel_spec(P[n], (l,)) for n in names],
        out_specs=row_spec(D_MODEL),
        out_shape=jax.ShapeDtypeStruct((rows, D_MODEL), F32),
        compiler_params=_params(("parallel",)),
        name="gated_merge",
    )(x, mods, P["g_pre"], P["g_post"], loc, att, *[P[n] for n in names])


def _rope_swap_perm():
    j = np.arange(QK_ROPE)
    return np.where((j % 16) < 8, j + 8, j - 8)


def _prepare(w_in, w_uq, w_ukv, w_pool, w_sgu, b_sgu):
    perm = _rope_swap_perm()
    zp, zu, zv, cq, ckv, kr, zb, zc, zx = jnp.split(
        w_in, np.cumsum((256, 256, 256, 256, 128, 32, 256, 256)), axis=2)
    pad_rope = lambda a: jnp.pad(a, ((0, 0), (0, 0), ROPE_PAD))
    main = [zp, zu, zv, zb, zc, zx, cq, ckv, pad_rope(kr)]
    win_ctx = jnp.concatenate(main, axis=2).astype(BF16)
    win_lat = jnp.concatenate(main + [pad_rope(kr[:, :, perm])], axis=2).astype(BF16)

    uq = w_uq.reshape(DEPTH, Q_LORA, MLA_HEADS, QK_NOPE + QK_ROPE)
    wq = jnp.pad(uq, ((0, 0), (0, 0), (0, 0), (0, HEAD_PAD - QK_NOPE - QK_ROPE)))
    wq_sw = jnp.pad(uq[:, :, :, QK_NOPE:][:, :, :, perm], ((0, 0), (0, 0), (0, 0), ROPE_PAD))
    to_t = lambda a: a.reshape(DEPTH, Q_LORA, QK_PAD).transpose(0, 2, 1)
    wq_ctx = wq.reshape(DEPTH, Q_LORA, QK_PAD).astype(BF16)
    wqt_lat = jnp.concatenate([to_t(wq), to_t(wq_sw)], axis=1).astype(BF16)

    ukv = w_ukv.reshape(DEPTH, KV_LORA, MLA_HEADS, QK_NOPE + V_HEAD)
    wk = jnp.pad(ukv[:, :, :, :QK_NOPE], ((0, 0), (0, 0), (0, 0), (0, HEAD_PAD - QK_NOPE)))
    wk = wk.reshape(DEPTH, KV_LORA, QK_PAD).astype(BF16)
    wv = ukv[:, :, :, QK_NOPE:].reshape(DEPTH, KV_LORA, MLA_W).astype(BF16)
    wvt = wv.transpose(0, 2, 1)

    eye = jnp.eye(POOL_GROUPS, dtype=F32)
    wpool = jnp.einsum("lgcd,gh->lgchd", w_pool, eye).reshape(DEPTH, POOL_W, POOL_W).astype(BF16)
    wsgu = w_sgu.reshape(DEPTH, SGU_GROUPS * CHUNK, CHUNK).astype(BF16)
    bsgu = jnp.repeat(b_sgu.transpose(0, 2, 1), SGU_W // SGU_GROUPS, axis=2)
    return dict(win_ctx=win_ctx, win_lat=win_lat, wq_ctx=wq_ctx, wqt_lat=wqt_lat, wk=wk, wv=wv, wvt=wvt,
                wpool=wpool, wsgu=wsgu, bsgu=bsgu, band=_pool_band())


def _rope_tables(n):
    rows = n // GRID_W
    r = jnp.broadcast_to(jnp.arange(rows)[:, None], (rows, GRID_W)).reshape(-1).astype(F32)
    col = jnp.broadcast_to(jnp.arange(GRID_W)[None, :], (rows, GRID_W)).reshape(-1).astype(F32)
    half = QK_ROPE // 2
    freqs = ROPE_THETA ** (-(2.0 * jnp.arange(half // 2, dtype=F32)) / half)
    ang = jnp.stack([r[:, None] * freqs, col[:, None] * freqs], axis=1)
    cos, sin = jnp.cos(ang), jnp.sin(ang)
    cos32 = jnp.concatenate([cos, cos], axis=-1).reshape(n, QK_ROPE)
    sin32 = jnp.concatenate([-sin, sin], axis=-1).reshape(n, QK_ROPE)
    cos_tab = jnp.concatenate([jnp.ones((n, QK_NOPE), F32), cos32,
                               jnp.zeros((n, HEAD_PAD - QK_NOPE - QK_ROPE), F32)], axis=1)
    sin_tab = jnp.pad(sin32, ((0, 0), ROPE_PAD))
    return cos_tab, sin_tab, cos_tab.T, sin_tab.T


def kernel(x_prompt, x_sample, cache_ckv, cache_krope, c, c_ctx, w_mod, b_mod, g_pre, g_post,
           w_ffn_gu, w_ffn_dn, w_in, w_pool, pool_scale, g_sgu, w_sgu, b_sgu, g_q, w_uq, g_kv,
           w_ukv, conv_w, w_br_pool, w_br_sgu, w_br_mla, w_br_conv, w_gate, b_gate, w_o):
    batch, seq, _ = x_prompt.shape
    dec_batch, dec_seq, _ = x_sample.shape

    cond8 = jnp.concatenate([c_ctx[None, :], c, jnp.zeros((8 - 1 - dec_batch, D_MODEL), F32)], axis=0)
    mods = _mods_call(cond8, w_mod, b_mod).reshape(DEPTH, 8, N_MOD, D_MODEL)
    rope_tabs = _rope_tables(dec_seq)
    place = jnp.pad(jnp.eye(QK_ROPE, dtype=F32), ((0, 0), ROPE_PAD)).astype(BF16)

    P = _prepare(w_in, w_uq, w_ukv, w_pool, w_sgu, b_sgu)
    row = lambda a: a.reshape(a.shape[:-1] + (1, a.shape[-1]))
    P.update(
        g_pre=row(g_pre), g_post=row(g_post), g_q=row(g_q), g_kv=row(g_kv), g_sgu=row(g_sgu),
        pool_scale=row(pool_scale), b_gate=row(b_gate), conv_w=conv_w,
        w_ffn_gu=w_ffn_gu.astype(BF16), w_ffn_dn=w_ffn_dn.astype(BF16), w_gate=w_gate.astype(BF16),
        w_br_pool=w_br_pool.astype(BF16), w_br_sgu=w_br_sgu.astype(BF16),
        w_br_mla=w_br_mla.astype(BF16), w_br_conv=w_br_conv.astype(BF16), w_o=w_o.astype(BF16),
    )

    streams = {"ctx": (x_prompt.reshape(batch * seq, D_MODEL), (0, 1), batch, seq),
               "lat": (x_sample.reshape(dec_batch * dec_seq, D_MODEL), (1, dec_batch), dec_batch, dec_seq)}
    outs = {}
    ckv_states, kr_states = [], []
    for name, (x, cond, nb, ns) in streams.items():
        is_ctx = name == "ctx"
        for l in range(DEPTH):
            x = _ffn_call(x, mods, cond, P, l, 0)
            front = _front_call(x, mods, cond, P, l, None if is_ctx else rope_tabs)
            zloc, q, k, v = front[:4]
            loc = _local_call(zloc, ns, P, l)
            if is_ctx:
                ckv_states.append(front[4].reshape(batch, seq, KV_LORA))
                kr_states.append(front[5].reshape(batch, seq, QK_ROPE))
                att = _attn_short_call(q, k, v, ns)
            else:
                att = _attn_call(q, [_kvup_call(cache_ckv, cache_krope, P, l, place), (k, v)], nb)
            x = _merge_call(x, mods, cond, P, l, loc, att)
            x = _ffn_call(x, mods, cond, P, l, 2)
        outs[name] = x

    y_prompt = outs["ctx"].reshape(batch, seq, D_MODEL)
    y_sample = outs["lat"].reshape(dec_batch, dec_seq, D_MODEL)
    state_ckv = jnp.stack(ckv_states, axis=1)
    state_krope = jnp.stack(kr_states, axis=1)
    return (y_prompt, y_sample, state_ckv, state_krope)
```

```python
import functools

import numpy as np
import jax
import jax.numpy as jnp
from jax import lax
from jax.experimental import pallas as pl
from jax.experimental.pallas import tpu as pltpu

F32 = jnp.float32
BF16 = jnp.bfloat16

D_MODEL = 1024
DEPTH = 2
GRID_W = 64
N_MOD = 9
D_FF = 2816
EPS = 1e-6
POOL_GROUPS = 4
POOL_GW = 64
POOL_W = 256
SGU_GROUPS = 4
SGU_W = 256
CHUNK = 128
MLA_HEADS = 8
QK_NOPE = 64
QK_ROPE = 32
V_HEAD = 64
Q_LORA = 256
KV_LORA = 128
MLA_W = MLA_HEADS * V_HEAD
ROPE_THETA = 10000.0
CONV_W = 256
CONV_K = 3
N_BRANCH = 4
HEAD_PAD = 128
QK_PAD = MLA_HEADS * HEAD_PAD
ROPE_PAD = (QK_NOPE, HEAD_PAD - QK_NOPE - QK_ROPE)
LOC_W = 6 * 256
MAIN_W = LOC_W + Q_LORA + KV_LORA
HALO = 8
POOL_HALF = (1, 2, 4, 8)
LOG2E = 1.4426950408889634

VMEM_LIMIT = 56 * 1024 * 1024

TM_TOKEN = 1024
SUB_TILES = 4
TM_LOCAL = 1024
TQ_ATTN = 256


def _sel_spec(arr, idx=()):
    rest = arr.shape[len(idx):]
    zeros = (0,) * len(rest)
    return pl.BlockSpec((None,) * len(idx) + rest, lambda *_: tuple(idx) + zeros,
                        pipeline_mode=pl.Buffered(1))


def _mods_spec(l, cond0, per_cond):
    return pl.BlockSpec((None, None, N_MOD, D_MODEL), lambda i: (l, cond0 + i // per_cond, 0, 0))


def _params(sem):
    return pltpu.CompilerParams(dimension_semantics=sem, vmem_limit_bytes=VMEM_LIMIT)


def _sigmoid(x):
    return 1.0 / (1.0 + jnp.exp(-x))


def _rms(x):
    return x * lax.rsqrt(jnp.mean(x * x, axis=-1, keepdims=True) + EPS)


def _pre(x, g, mods_ref, s):
    return (_rms(x) * g) * (1.0 + mods_ref[3 * s + 1:3 * s + 2, :]) + mods_ref[3 * s:3 * s + 1, :]


def _dot(a, b):
    return jnp.dot(a, b, preferred_element_type=F32)


def _dot_nt(a, b):
    return lax.dot_general(a, b, (((1,), (1,)), ((), ())), preferred_element_type=F32)


def _mods_kernel(cond_ref, w_ref, b_ref, o_ref):
    c = cond_ref[...]
    a = (c * _sigmoid(c)).astype(BF16)
    o_ref[0] = _dot(a, w_ref[0].astype(BF16)) + b_ref[0]


def _mods_call(cond8, w_mod, b_mod):
    n_tile = 1024
    nt = (N_MOD * D_MODEL) // n_tile
    return pl.pallas_call(
        _mods_kernel,
        grid=(DEPTH, nt),
        in_specs=[
            pl.BlockSpec((8, D_MODEL), lambda l, j: (0, 0)),
            pl.BlockSpec((1, D_MODEL, n_tile), lambda l, j: (l, 0, j)),
            pl.BlockSpec((1, 1, n_tile), lambda l, j: (l, 0, j)),
        ],
        out_specs=pl.BlockSpec((1, 8, n_tile), lambda l, j: (l, 0, j)),
        out_shape=jax.ShapeDtypeStruct((DEPTH, 8, N_MOD * D_MODEL), F32),
        compiler_params=_params(("parallel", "parallel")),
        name="adaln_mods",
    )(cond8, w_mod, b_mod.reshape(DEPTH, 1, N_MOD * D_MODEL))


def _sub_tiles(tm):
    step = tm // SUB_TILES
    return [slice(r * step, (r + 1) * step) for r in range(SUB_TILES)]


def _ffn_kernel(x_ref, mods_ref, gpre_ref, gpost_ref, wgu_ref, wdn_ref, o_ref, *, s):
    rows = _sub_tiles(x_ref.shape[0])
    gate = mods_ref[3 * s + 2:3 * s + 3, :]

    def pre(r):
        return _pre(x_ref[rows[r], :], gpre_ref[...], mods_ref, s).astype(BF16)

    def post(r, f):
        o_ref[rows[r], :] = x_ref[rows[r], :] + (0.5 * gate) * (_rms(f) * gpost_ref[...])

    h = pre(0)
    f_prev = None
    for r in range(SUB_TILES):
        g = _dot(h, wgu_ref[:, :D_FF])
        u = _dot(h, wgu_ref[:, D_FF:])
        a = ((g * _sigmoid(g)) * u).astype(BF16)
        if r + 1 < SUB_TILES:
            h = pre(r + 1)
        f = _dot(a, wdn_ref[...])
        if r > 0:
            post(r - 1, f_prev)
        f_prev = f
    post(SUB_TILES - 1, f_prev)


def _ffn_call(x, mods, cond, P, l, s):
    rows = x.shape[0]
    tm = TM_TOKEN
    j = s // 2
    return pl.pallas_call(
        functools.partial(_ffn_kernel, s=s),
        grid=(rows // tm,),
        in_specs=[
            pl.BlockSpec((tm, D_MODEL), lambda i: (i, 0)),
            _mods_spec(l, cond[0], rows // cond[1] // tm),
            _sel_spec(P["g_pre"], (l, s)),
            _sel_spec(P["g_post"], (l, s)),
            _sel_spec(P["w_ffn_gu"], (l, j)),
            _sel_spec(P["w_ffn_dn"], (l, j)),
        ],
        out_specs=pl.BlockSpec((tm, D_MODEL), lambda i: (i, 0)),
        out_shape=jax.ShapeDtypeStruct((rows, D_MODEL), F32),
        compiler_params=_params(("parallel",)),
        name="ffn_half_step",
    )(x, mods, P["g_pre"], P["g_post"], P["w_ffn_gu"], P["w_ffn_dn"])


def _front_kernel(*refs, latent):
    if latent:
        (x_ref, mods_ref, gpre_ref, win_ref, gq_ref, wq_ref, gkv_ref, wk_ref, wv_ref,
         cos_ref, sin_ref, cost_ref, sint_ref, zloc_ref, q_ref, k_ref, v_ref) = refs
    else:
        (x_ref, mods_ref, gpre_ref, win_ref, gq_ref, wq_ref, gkv_ref, wk_ref, wv_ref,
         zloc_ref, q_ref, k_ref, v_ref, ckv_ref, kr_ref) = refs
    q_scale = float(QK_NOPE + QK_ROPE) ** -0.5 * LOG2E
    rows = _sub_tiles(x_ref.shape[0])

    def pre(r):
        return _pre(x_ref[rows[r], :], gpre_ref[...], mods_ref, 1).astype(BF16)

    def project(rs, z):
        zloc_ref[rs, :] = z[:, :LOC_W]
        qn = (_rms(z[:, LOC_W:LOC_W + Q_LORA]) * gq_ref[...]).astype(BF16)
        ckv_n = _rms(z[:, LOC_W + Q_LORA:MAIN_W]) * gkv_ref[...]
        kr = z[:, MAIN_W:MAIN_W + HEAD_PAD]
        ckv_b = ckv_n.astype(BF16)
        kn = _dot(ckv_b, wk_ref[...])
        if latent:
            kr = kr * cos_ref[rs, :] + z[:, MAIN_W + HEAD_PAD:] * sin_ref[rs, :]
            vt = _dot_nt(wv_ref[...], ckv_b)
            qt = _dot_nt(wq_ref[0:QK_PAD, :], qn)
            qt_sw = _dot_nt(wq_ref[QK_PAD:, :], qn)
            cost = cost_ref[:, rs]
            sint = sint_ref[:, rs]
            for hd in range(MLA_HEADS):
                sl = slice(hd * HEAD_PAD, (hd + 1) * HEAD_PAD)
                q_ref[hd, :, rs] = ((qt[sl, :] * cost + qt_sw[sl, :] * sint) * q_scale).astype(BF16)
                k_ref[hd, rs, :] = (kn[:, sl] + kr).astype(BF16)
                v_ref[hd, :, rs] = vt[hd * V_HEAD:(hd + 1) * V_HEAD, :].astype(BF16)
        else:
            ckv_ref[rs, :] = ckv_n
            kr_ref[rs, :] = kr[:, QK_NOPE:QK_NOPE + QK_ROPE]
            v_ref[rs, :] = _dot(ckv_b, wv_ref[...]).astype(BF16)
            q_ref[rs, :] = (_dot(qn, wq_ref[...]) * q_scale).astype(BF16)
            for hd in range(MLA_HEADS):
                sl = slice(hd * HEAD_PAD, (hd + 1) * HEAD_PAD)
                k_ref[rs, sl] = (kn[:, sl] + kr).astype(BF16)

    h = pre(0)
    for r in range(SUB_TILES):
        z = _dot(h, win_ref[...])
        if r + 1 < SUB_TILES:
            h = pre(r + 1)
        project(rows[r], z)


def _front_call(x, mods, cond, P, l, rope_tabs):
    rows = x.shape[0]
    tm = TM_TOKEN
    latent = rope_tabs is not None
    win = P["win_lat"] if latent else P["win_ctx"]
    wq = P["wqt_lat"] if latent else P["wq_ctx"]
    wv = P["wvt"] if latent else P["wv"]
    row_spec = lambda w: pl.BlockSpec((tm, w), lambda i: (i, 0))
    in_specs = [
        row_spec(D_MODEL),
        _mods_spec(l, cond[0], rows // cond[1] // tm),
        _sel_spec(P["g_pre"], (l, 1)),
        _sel_spec(win, (l,)),
        _sel_spec(P["g_q"], (l,)),
        _sel_spec(wq, (l,)),
        _sel_spec(P["g_kv"], (l,)),
        _sel_spec(P["wk"], (l,)),
        _sel_spec(wv, (l,)),
    ]
    args = [x, mods, P["g_pre"], win, P["g_q"], wq, P["g_kv"], P["wk"], wv]
    if latent:
        cos_tab, sin_tab, cos_tab_t, sin_tab_t = rope_tabs
        seq_tiles = cos_tab.shape[0] // tm
        in_specs += [pl.BlockSpec((tm, HEAD_PAD), lambda i: (i % seq_tiles, 0))] * 2
        in_specs += [pl.BlockSpec((HEAD_PAD, tm), lambda i: (0, i % seq_tiles))] * 2
        args += [cos_tab, sin_tab, cos_tab_t, sin_tab_t]
        out_specs = [
            row_spec(LOC_W),
            pl.BlockSpec((MLA_HEADS, HEAD_PAD, tm), lambda i: (0, 0, i)),
            pl.BlockSpec((MLA_HEADS, tm, HEAD_PAD), lambda i: (0, i, 0)),
            pl.BlockSpec((MLA_HEADS, V_HEAD, tm), lambda i: (0, 0, i)),
        ]
        out_shape = [
            jax.ShapeDtypeStruct((rows, LOC_W), F32),
            jax.ShapeDtypeStruct((MLA_HEADS, HEAD_PAD, rows), BF16),
            jax.ShapeDtypeStruct((MLA_HEADS, rows, HEAD_PAD), BF16),
            jax.ShapeDtypeStruct((MLA_HEADS, V_HEAD, rows), BF16),
        ]
    else:
        out_specs = [row_spec(LOC_W), row_spec(QK_PAD), row_spec(QK_PAD), row_spec(MLA_W),
                     row_spec(KV_LORA), row_spec(QK_ROPE)]
        out_shape = [
            jax.ShapeDtypeStruct((rows, LOC_W), F32),
            jax.ShapeDtypeStruct((rows, QK_PAD), BF16),
            jax.ShapeDtypeStruct((rows, QK_PAD), BF16),
            jax.ShapeDtypeStruct((rows, MLA_W), BF16),
            jax.ShapeDtypeStruct((rows, KV_LORA), F32),
            jax.ShapeDtypeStruct((rows, QK_ROPE), F32),
        ]
    return pl.pallas_call(
        functools.partial(_front_kernel, latent=latent),
        grid=(rows // tm,),
        in_specs=in_specs,
        out_specs=out_specs,
        out_shape=out_shape,
        compiler_params=_params(("parallel",)),
        name="mix_front_latent" if latent else "mix_front_context",
    )(*args)


def _kvup_kernel(ckv_ref, kr_ref, wk_ref, wvt_ref, place_ref, k_ref, vt_ref):
    ckv_b = ckv_ref[...].astype(BF16)
    kn = _dot(ckv_b, wk_ref[...])
    kr = _dot(kr_ref[...].astype(BF16), place_ref[...])
    vt = _dot_nt(wvt_ref[...], ckv_b)
    for hd in range(MLA_HEADS):
        sl = slice(hd * HEAD_PAD, (hd + 1) * HEAD_PAD)
        k_ref[hd] = (kn[:, sl] + kr).astype(BF16)
        vt_ref[hd] = vt[hd * V_HEAD:(hd + 1) * V_HEAD, :].astype(BF16)


def _kvup_call(cache_ckv, cache_krope, P, l, place):
    b, _, n, _ = cache_ckv.shape
    return pl.pallas_call(
        _kvup_kernel,
        grid=(b,),
        in_specs=[
            pl.BlockSpec((None, None, n, KV_LORA), lambda i: (i, l, 0, 0)),
            pl.BlockSpec((None, None, n, QK_ROPE), lambda i: (i, l, 0, 0)),
            _sel_spec(P["wk"], (l,)),
            _sel_spec(P["wvt"], (l,)),
            _sel_spec(place),
        ],
        out_specs=[pl.BlockSpec((None, MLA_HEADS, n, HEAD_PAD), lambda i: (i, 0, 0, 0)),
                   pl.BlockSpec((None, MLA_HEADS, V_HEAD, n), lambda i: (i, 0, 0, 0))],
        out_shape=[jax.ShapeDtypeStruct((b, MLA_HEADS, n, HEAD_PAD), BF16),
                   jax.ShapeDtypeStruct((b, MLA_HEADS, V_HEAD, n), BF16)],
        compiler_params=_params(("parallel",)),
        name="cache_kv_up",
    )(cache_ckv, cache_krope, P["wk"], P["wvt"], place)


def _by_group(grp, r):
    return jnp.where(grp == 0, r[0:CHUNK],
                     jnp.where(grp == 1, r[CHUNK:2 * CHUNK],
                               jnp.where(grp == 2, r[2 * CHUNK:3 * CHUNK], r[3 * CHUNK:])))


def _local_kernel(z_ref, zprev_ref, znext_ref, band_ref, wpool_ref, pscale_ref, gsgu_ref, wsgu_ref,
                  bsgu_ref, convw_ref, o_ref, ext_ref, *, seqlen, tm):
    i = pl.program_id(0)
    starts_seq = (i * tm) % seqlen == 0
    ends_seq = ((i + 1) * tm) % seqlen == 0
    t = (lax.broadcasted_iota(jnp.int32, (tm, 1), 0) + i * tm) & (seqlen - 1)
    lane = lax.broadcasted_iota(jnp.int32, (1, POOL_W), 1)
    grp = lane // POOL_GW

    zp = z_ref[:, 0:256]
    zu = z_ref[:, 256:512]
    zv = z_ref[:, 512:768]
    zb = z_ref[:, 768:1024]
    prod = z_ref[:, 1024:1280] * z_ref[:, 1280:1536]

    ext_ref[0:HALO, 0:256] = jnp.where(starts_seq, 0.0, zprev_ref[:, 0:256])
    ext_ref[0:HALO, 256:512] = jnp.where(starts_seq, 0.0, zprev_ref[:, 1024:1280] * zprev_ref[:, 1280:1536])
    ext_ref[HALO:HALO + tm, 0:256] = zp
    ext_ref[HALO:HALO + tm, 256:512] = prod
    ext_ref[HALO + tm:2 * HALO + tm, 0:256] = jnp.where(ends_seq, 0.0, znext_ref[:, 0:256])
    ext_ref[HALO + tm:2 * HALO + tm, 256:512] = jnp.where(
        ends_seq, 0.0, znext_ref[:, 1024:1280] * znext_ref[:, 1280:1536])
    ext_ref[2 * HALO + tm:, :] = jnp.zeros((CHUNK - 2 * HALO, 512), F32)

    xe = ext_ref[:, 0:256]
    hi = xe.astype(BF16)
    lo = (xe - hi.astype(F32)).astype(BF16)
    band = band_ref[...]
    sums = []
    for n in range(tm // CHUNK):
        ws = slice(n * CHUNK, n * CHUNK + 2 * CHUNK)
        sums.append(_by_group(grp, _dot(band, hi[ws, :]) + _dot(band, lo[ws, :])))
    wsum = jnp.concatenate(sums, axis=0) if len(sums) > 1 else sums[0]
    half = jnp.where(grp == 0, POOL_HALF[0],
                     jnp.where(grp == 1, POOL_HALF[1],
                               jnp.where(grp == 2, POOL_HALF[2], POOL_HALF[3])))
    cnt = (jnp.minimum(t + half, seqlen) - jnp.maximum(t - half, 0)).astype(F32)
    dpool = (wsum / cnt - zp).astype(BF16)
    o_ref[:, 0:256] = (_dot(dpool, wpool_ref[...]) * pscale_ref[...]).astype(BF16)

    vc = (_rms(zv) * gsgu_ref[...]).astype(BF16)
    wsgu = wsgu_ref[...]
    bias = bsgu_ref[...]
    for n in range(tm // CHUNK):
        rs = slice(n * CHUNK, (n + 1) * CHUNK)
        mixed = _by_group(grp, _dot(wsgu, vc[rs, :]))
        o_ref[rs, 256:512] = (zu[rs, :] * (mixed + bias)).astype(BF16)

    pm1 = ext_ref[HALO - 1:HALO - 1 + tm, 256:512]
    pp1 = ext_ref[HALO + 1:HALO + 1 + tm, 256:512]
    conv = pm1 * convw_ref[0:1, :] + prod * convw_ref[1:2, :] + pp1 * convw_ref[2:3, :]
    o_ref[:, 512:768] = (zb * conv).astype(BF16)


def _pool_band():
    p = np.arange(CHUNK)[:, None]
    j = np.arange(2 * CHUNK)[None, :]
    bands = [((j >= p + HALO - hw) & (j < p + HALO + hw)) for hw in POOL_HALF]
    return jnp.asarray(np.concatenate(bands, axis=0).astype(np.float32), dtype=BF16)


def _local_call(zloc, seqlen, P, l):
    rows = zloc.shape[0]
    tm = min(TM_LOCAL, seqlen)
    assert seqlen % tm == 0 and tm % CHUNK == 0
    hb = tm // HALO
    last = rows // HALO - 1
    names = ("wpool", "pool_scale", "g_sgu", "wsgu", "bsgu", "conv_w")
    return pl.pallas_call(
        functools.partial(_local_kernel, seqlen=seqlen, tm=tm),
        grid=(rows // tm,),
        in_specs=[
            pl.BlockSpec((tm, LOC_W), lambda i: (i, 0)),
            pl.BlockSpec((HALO, LOC_W), lambda i: (jnp.maximum(i * hb - 1, 0), 0)),
            pl.BlockSpec((HALO, LOC_W), lambda i: (jnp.minimum((i + 1) * hb, last), 0)),
            _sel_spec(P["band"]),
        ] + [_sel_spec(P[n], (l,)) for n in names],
        out_specs=pl.BlockSpec((tm, 768), lambda i: (i, 0)),
        out_shape=jax.ShapeDtypeStruct((rows, 768), BF16),
        scratch_shapes=[pltpu.VMEM((tm + CHUNK, 512), F32)],
        compiler_params=_params(("parallel",)),
        name="local_mixers",
    )(zloc, zloc, zloc, P["band"], *[P[n] for n in names])


def _attn_kernel(*refs, n_seg):
    qt_ref, qtn_ref = refs[0:2]
    k_refs = refs[2:2 + n_seg]
    vt_refs = refs[2 + n_seg:2 + 2 * n_seg]
    o_ref, sa_ref, sb_ref, ma_ref, mb_ref, ot_ref = refs[2 + 2 * n_seg:]
    buf_a = (sa_ref, ma_ref)
    buf_b = (sb_ref, mb_ref)
    bounds = [0]
    for k_ref in k_refs:
        bounds.append(bounds[-1] + k_ref.shape[1])

    def scores(q_ref, h, buf):
        s_ref, m_ref = buf
        qt = q_ref[h]
        m = None
        for k_ref, lo, hi in zip(k_refs, bounds[:-1], bounds[1:]):
            s = _dot(k_ref[h], qt)
            s_ref[lo:hi, :] = s
            ms = jnp.max(s, axis=0, keepdims=True)
            m = ms if m is None else jnp.maximum(m, ms)
        m_ref[...] = m

    def values(h, buf):
        s_ref, m_ref = buf
        m = m_ref[...]
        l = None
        ot = None
        for vt_ref, lo, hi in zip(vt_refs, bounds[:-1], bounds[1:]):
            p = jnp.exp2(s_ref[lo:hi, :] - m)
            ls = jnp.sum(p, axis=0, keepdims=True)
            os_ = _dot(vt_ref[h], p.astype(BF16))
            l = ls if l is None else l + ls
            ot = os_ if ot is None else ot + os_
        ot_ref[pl.ds(pl.multiple_of(h * V_HEAD, V_HEAD), V_HEAD), :] = ot / l

    @pl.when(pl.program_id(1) == 0)
    def _():
        scores(qt_ref, 0, buf_a)

    def body(h, carry):
        @pl.when(h % 2 == 0)
        def _():
            scores(qt_ref, h + 1, buf_b)
            values(h, buf_a)

        @pl.when(h % 2 == 1)
        def _():
            scores(qt_ref, h + 1, buf_a)
            values(h, buf_b)
        return carry

    lax.fori_loop(0, MLA_HEADS - 1, body, 0)
    scores(qtn_ref, 0, buf_a)
    values(MLA_HEADS - 1, buf_b)
    o_ref[...] = ot_ref[...].T.astype(BF16)


def _attn_call(qt, segs, batch):
    rows = qt.shape[2]
    lq = rows // batch
    tq = TQ_ATTN
    nq = lq // tq
    n_seg = len(segs)
    k_specs, vt_specs, k_args, vt_args, keys = [], [], [], [], 0
    for k, vt in segs:
        if k.ndim == 4:
            n = k.shape[2]
            k_specs.append(pl.BlockSpec((None, MLA_HEADS, n, HEAD_PAD), lambda i, j: (i, 0, 0, 0)))
            vt_specs.append(pl.BlockSpec((None, MLA_HEADS, V_HEAD, n), lambda i, j: (i, 0, 0, 0)))
        else:
            n = k.shape[1] // batch
            k_specs.append(pl.BlockSpec((MLA_HEADS, n, HEAD_PAD), lambda i, j: (0, i, 0)))
            vt_specs.append(pl.BlockSpec((MLA_HEADS, V_HEAD, n), lambda i, j: (0, 0, i)))
        k_args.append(k)
        vt_args.append(vt)
        keys += n
    assert MLA_HEADS % 2 == 0
    return pl.pallas_call(
        functools.partial(_attn_kernel, n_seg=n_seg),
        grid=(batch, nq),
        in_specs=[
            pl.BlockSpec((MLA_HEADS, HEAD_PAD, tq), lambda i, j: (0, 0, i * nq + j)),
            pl.BlockSpec((MLA_HEADS, HEAD_PAD, tq), lambda i, j: (0, 0, i * nq + jnp.minimum(j + 1, nq - 1))),
        ] + k_specs + vt_specs,
        out_specs=pl.BlockSpec((tq, MLA_W), lambda i, j: (i * nq + j, 0)),
        out_shape=jax.ShapeDtypeStruct((rows, MLA_W), BF16),
        scratch_shapes=[pltpu.VMEM((keys, tq), F32), pltpu.VMEM((keys, tq), F32),
                        pltpu.VMEM((1, tq), F32), pltpu.VMEM((1, tq), F32),
                        pltpu.VMEM((MLA_W, tq), F32)],
        compiler_params=_params(("arbitrary", "arbitrary")),
        name="mla_attention",
    )(qt, qt, *k_args, *vt_args)


def _attn_short_kernel(q_ref, k_ref, v_ref, o_ref):
    lane = lax.broadcasted_iota(jnp.int32, (1, HEAD_PAD), 1)
    for j in range(MLA_HEADS // 2):
        vs = v_ref[:, j * HEAD_PAD:(j + 1) * HEAD_PAD]
        outs = []
        for hd in (2 * j, 2 * j + 1):
            sl = slice(hd * HEAD_PAD, (hd + 1) * HEAD_PAD)
            s = _dot_nt(q_ref[:, sl], k_ref[:, sl])
            p = jnp.exp2(s - jnp.max(s, axis=-1, keepdims=True))
            l = jnp.sum(p, axis=-1, keepdims=True)
            outs.append(_dot(p.astype(BF16), vs) / l)
        o_ref[:, j * HEAD_PAD:(j + 1) * HEAD_PAD] = jnp.where(lane < V_HEAD, outs[0], outs[1]).astype(BF16)


def _attn_short_call(q, k, v, seq):
    rows = q.shape[0]
    return pl.pallas_call(
        _attn_short_kernel,
        grid=(rows // seq,),
        in_specs=[
            pl.BlockSpec((seq, QK_PAD), lambda i: (i, 0)),
            pl.BlockSpec((seq, QK_PAD), lambda i: (i, 0)),
            pl.BlockSpec((seq, MLA_W), lambda i: (i, 0)),
        ],
        out_specs=pl.BlockSpec((seq, MLA_W), lambda i: (i, 0)),
        out_shape=jax.ShapeDtypeStruct((rows, MLA_W), BF16),
        compiler_params=_params(("parallel",)),
        name="mla_attention_context",
    )(q, k, v)


def _merge_kernel(x_ref, mods_ref, gpre_ref, gpost_ref, loc_ref, att_ref, wgate_ref, bgate_ref,
                  wbp_ref, wbs_ref, wbm_ref, wbc_ref, wo_ref, o_ref):
    rows = _sub_tiles(x_ref.shape[0])

    def pre(r):
        return _pre(x_ref[rows[r], :], gpre_ref[...], mods_ref, 1).astype(BF16)

    def post(r, m):
        o_ref[rows[r], :] = x_ref[rows[r], :] + mods_ref[5:6, :] * (_rms(m) * gpost_ref[...])

    h = pre(0)
    m_prev = None
    for r in range(SUB_TILES):
        rs = rows[r]
        branches = (
            (loc_ref[rs, 0:256], wbp_ref),
            (loc_ref[rs, 256:512], wbs_ref),
            (att_ref[rs, :], wbm_ref),
            (loc_ref[rs, 512:768], wbc_ref),
        )
        merged = None
        for j, (inp, w_ref) in enumerate(branches):
            sl = slice(j * D_MODEL, (j + 1) * D_MODEL)
            gate = _sigmoid(_dot(h, wgate_ref[:, sl]) + bgate_ref[:, sl])
            term = gate * _dot(inp, w_ref[...])
            merged = term if merged is None else merged + term
        if r + 1 < SUB_TILES:
            h = pre(r + 1)
        m = _dot(merged.astype(BF16), wo_ref[...])
        if r > 0:
            post(r - 1, m_prev)
        m_prev = m
    post(SUB_TILES - 1, m_prev)


def _merge_call(x, mods, cond, P, l, loc, att):
    rows = x.shape[0]
    tm = TM_TOKEN
    row_spec = lambda w: pl.BlockSpec((tm, w), lambda i: (i, 0))
    names = ("w_gate", "b_gate", "w_br_pool", "w_br_sgu", "w_br_mla", "w_br_conv", "w_o")
    return pl.pallas_call(
        _merge_kernel,
        grid=(rows // tm,),
        in_specs=[
            row_spec(D_MODEL),
            _mods_spec(l, cond[0], rows // cond[1] // tm),
            _sel_spec(P["g_pre"], (l, 1)),
            _sel_spec(P["g_post"], (l, 1)),
            row_spec(768),
            row_spec(MLA_W),
        ] + [_sel_spec(P[n], (l,)) for n in names],
        out_specs=row_spec(D_MODEL),
        out_shape=jax.ShapeDtypeStruct((rows, D_MODEL), F32),
        compiler_params=_params(("parallel",)),
        name="gated_merge",
    )(x, mods, P["g_pre"], P["g_post"], loc, att, *[P[n] for n in names])


def _rope_swap_perm():
    j = np.arange(QK_ROPE)
    return np.where((j % 16) < 8, j + 8, j - 8)


def _prepare(w_in, w_uq, w_ukv, w_pool, w_sgu, b_sgu):
    perm = _rope_swap_perm()
    zp, zu, zv, cq, ckv, kr, zb, zc, zx = jnp.split(
        w_in, np.cumsum((256, 256, 256, 256, 128, 32, 256, 256)), axis=2)
    pad_rope = lambda a: jnp.pad(a, ((0, 0), (0, 0), ROPE_PAD))
    main = [zp, zu, zv, zb, zc, zx, cq, ckv, pad_rope(kr)]
    win_ctx = jnp.concatenate(main, axis=2).astype(BF16)
    win_lat = jnp.concatenate(main + [pad_rope(kr[:, :, perm])], axis=2).astype(BF16)

    uq = w_uq.reshape(DEPTH, Q_LORA, MLA_HEADS, QK_NOPE + QK_ROPE)
    wq = jnp.pad(uq, ((0, 0), (0, 0), (0, 0), (0, HEAD_PAD - QK_NOPE - QK_ROPE)))
    wq_sw = jnp.pad(uq[:, :, :, QK_NOPE:][:, :, :, perm], ((0, 0), (0, 0), (0, 0), ROPE_PAD))
    to_t = lambda a: a.reshape(DEPTH, Q_LORA, QK_PAD).transpose(0, 2, 1)
    wq_ctx = wq.reshape(DEPTH, Q_LORA, QK_PAD).astype(BF16)
    wqt_lat = jnp.concatenate([to_t(wq), to_t(wq_sw)], axis=1).astype(BF16)

    ukv = w_ukv.reshape(DEPTH, KV_LORA, MLA_HEADS, QK_NOPE + V_HEAD)
    wk = jnp.pad(ukv[:, :, :, :QK_NOPE], ((0, 0), (0, 0), (0, 0), (0, HEAD_PAD - QK_NOPE)))
    wk = wk.reshape(DEPTH, KV_LORA, QK_PAD).astype(BF16)
    wv = ukv[:, :, :, QK_NOPE:].reshape(DEPTH, KV_LORA, MLA_W).astype(BF16)
    wvt = wv.transpose(0, 2, 1)

    eye = jnp.eye(POOL_GROUPS, dtype=F32)
    wpool = jnp.einsum("lgcd,gh->lgchd", w_pool, eye).reshape(DEPTH, POOL_W, POOL_W).astype(BF16)
    wsgu = w_sgu.reshape(DEPTH, SGU_GROUPS * CHUNK, CHUNK).astype(BF16)
    bsgu = jnp.repeat(b_sgu.transpose(0, 2, 1), SGU_W // SGU_GROUPS, axis=2)
    return dict(win_ctx=win_ctx, win_lat=win_lat, wq_ctx=wq_ctx, wqt_lat=wqt_lat, wk=wk, wv=wv, wvt=wvt,
                wpool=wpool, wsgu=wsgu, bsgu=bsgu, band=_pool_band())


def _rope_tables(n):
    rows = n // GRID_W
    r = jnp.broadcast_to(jnp.arange(rows)[:, None], (rows, GRID_W)).reshape(-1).astype(F32)
    col = jnp.broadcast_to(jnp.arange(GRID_W)[None, :], (rows, GRID_W)).reshape(-1).astype(F32)
    half = QK_ROPE // 2
    freqs = ROPE_THETA ** (-(2.0 * jnp.arange(half // 2, dtype=F32)) / half)
    ang = jnp.stack([r[:, None] * freqs, col[:, None] * freqs], axis=1)
    cos, sin = jnp.cos(ang), jnp.sin(ang)
    cos32 = jnp.concatenate([cos, cos], axis=-1).reshape(n, QK_ROPE)
    sin32 = jnp.concatenate([-sin, sin], axis=-1).reshape(n, QK_ROPE)
    cos_tab = jnp.concatenate([jnp.ones((n, QK_NOPE), F32), cos32,
                               jnp.zeros((n, HEAD_PAD - QK_NOPE - QK_ROPE), F32)], axis=1)
    sin_tab = jnp.pad(sin32, ((0, 0), ROPE_PAD))
    return cos_tab, sin_tab, cos_tab.T, sin_tab.T


def kernel(x_prompt, x_sample, cache_ckv, cache_krope, c, c_ctx, w_mod, b_mod, g_pre, g_post,
           w_ffn_gu, w_ffn_dn, w_in, w_pool, pool_scale, g_sgu, w_sgu, b_sgu, g_q, w_uq, g_kv,
           w_ukv, conv_w, w_br_pool, w_br_sgu, w_br_mla, w_br_conv, w_gate, b_gate, w_o):
    batch, seq, _ = x_prompt.shape
    dec_batch, dec_seq, _ = x_sample.shape

    cond8 = jnp.concatenate([c_ctx[None, :], c, jnp.zeros((8 - 1 - dec_batch, D_MODEL), F32)], axis=0)
    mods = _mods_call(cond8, w_mod, b_mod).reshape(DEPTH, 8, N_MOD, D_MODEL)
    rope_tabs = _rope_tables(dec_seq)
    place = jnp.pad(jnp.eye(QK_ROPE, dtype=F32), ((0, 0), ROPE_PAD)).astype(BF16)

    P = _prepare(w_in, w_uq, w_ukv, w_pool, w_sgu, b_sgu)
    row = lambda a: a.reshape(a.shape[:-1] + (1, a.shape[-1]))
    P.update(
        g_pre=row(g_pre), g_post=row(g_post), g_q=row(g_q), g_kv=row(g_kv), g_sgu=row(g_sgu),
        pool_scale=row(pool_scale), b_gate=row(b_gate), conv_w=conv_w,
        w_ffn_gu=w_ffn_gu.astype(BF16), w_ffn_dn=w_ffn_dn.astype(BF16), w_gate=w_gate.astype(BF16),
        w_br_pool=w_br_pool.astype(BF16), w_br_sgu=w_br_sgu.astype(BF16),
        w_br_mla=w_br_mla.astype(BF16), w_br_conv=w_br_conv.astype(BF16), w_o=w_o.astype(BF16),
    )

    streams = {"ctx": (x_prompt.reshape(batch * seq, D_MODEL), (0, 1), batch, seq),
               "lat": (x_sample.reshape(dec_batch * dec_seq, D_MODEL), (1, dec_batch), dec_batch, dec_seq)}
    outs = {}
    ckv_states, kr_states = [], []
    for name, (x, cond, nb, ns) in streams.items():
        is_ctx = name == "ctx"
        for l in range(DEPTH):
            x = _ffn_call(x, mods, cond, P, l, 0)
            front = _front_call(x, mods, cond, P, l, None if is_ctx else rope_tabs)
            zloc, q, k, v = front[:4]
            loc = _local_call(zloc, ns, P, l)
            if is_ctx:
                ckv_states.append(front[4].reshape(batch, seq, KV_LORA))
                kr_states.append(front[5].reshape(batch, seq, QK_ROPE))
                att = _attn_short_call(q, k, v, ns)
            else:
                att = _attn_call(q, [_kvup_call(cache_ckv, cache_krope, P, l, place), (k, v)], nb)
            x = _merge_call(x, mods, cond, P, l, loc, att)
            x = _ffn_call(x, mods, cond, P, l, 2)
        outs[name] = x

    y_prompt = outs["ctx"].reshape(batch, seq, D_MODEL)
    y_sample = outs["lat"].reshape(dec_batch, dec_seq, D_MODEL)
    state_ckv = jnp.stack(ckv_states, axis=1)
    state_krope = jnp.stack(kr_states, axis=1)
    return (y_prompt, y_sample, state_ckv, state_krope)
```

```python
import functools

import numpy as np
import jax
import jax.numpy as jnp
from jax import lax
from jax.experimental import pallas as pl
from jax.experimental.pallas import tpu as pltpu

F32 = jnp.float32
BF16 = jnp.bfloat16

D_MODEL = 1024
DEPTH = 2
GRID_W = 64
N_MOD = 9
D_FF = 2816
EPS = 1e-6
POOL_GROUPS = 4
POOL_GW = 64
POOL_W = 256
SGU_GROUPS = 4
SGU_W = 256
CHUNK = 128
MLA_HEADS = 8
QK_NOPE = 64
QK_ROPE = 32
V_HEAD = 64
Q_LORA = 256
KV_LORA = 128
MLA_W = MLA_HEADS * V_HEAD
ROPE_THETA = 10000.0
CONV_W = 256
CONV_K = 3
N_BRANCH = 4
HEAD_PAD = 128
QK_PAD = MLA_HEADS * HEAD_PAD
ROPE_PAD = (QK_NOPE, HEAD_PAD - QK_NOPE - QK_ROPE)
LOC_W = 6 * 256
MAIN_W = LOC_W + Q_LORA + KV_LORA
HALO = 8
POOL_HALF = (1, 2, 4, 8)
LOG2E = 1.4426950408889634

VMEM_LIMIT = 56 * 1024 * 1024

TM_TOKEN = 1024
SUB_ROWS = 256
TM_FRONT = 512
V_ROWS = V_HEAD + 16
TM_LOCAL = 1024
TQ_ATTN = 256


def _sel_spec(arr, idx=()):
    rest = arr.shape[len(idx):]
    zeros = (0,) * len(rest)
    return pl.BlockSpec((None,) * len(idx) + rest, lambda *_: tuple(idx) + zeros,
                        pipeline_mode=pl.Buffered(1))


def _mods_spec(l, cond0, per_cond):
    return pl.BlockSpec((None, None, N_MOD, D_MODEL), lambda i: (l, cond0 + i // per_cond, 0, 0))


def _params(sem):
    return pltpu.CompilerParams(dimension_semantics=sem, vmem_limit_bytes=VMEM_LIMIT)


def _sigmoid(x):
    return 1.0 / (1.0 + jnp.exp(-x))


def _rms(x):
    return x * lax.rsqrt(jnp.mean(x * x, axis=-1, keepdims=True) + EPS)


def _pre(x, g, mods_ref, s):
    return (_rms(x) * g) * (1.0 + mods_ref[3 * s + 1:3 * s + 2, :]) + mods_ref[3 * s:3 * s + 1, :]


def _dot(a, b):
    return jnp.dot(a, b, preferred_element_type=F32)


def _dot_nt(a, b):
    return lax.dot_general(a, b, (((1,), (1,)), ((), ())), preferred_element_type=F32)


def _mods_kernel(cond_ref, w_ref, b_ref, o_ref):
    c = cond_ref[...]
    a = (c * _sigmoid(c)).astype(BF16)
    o_ref[0] = _dot(a, w_ref[0].astype(BF16)) + b_ref[0]


def _mods_call(cond8, w_mod, b_mod):
    n_tile = 1024
    nt = (N_MOD * D_MODEL) // n_tile
    return pl.pallas_call(
        _mods_kernel,
        grid=(DEPTH, nt),
        in_specs=[
            pl.BlockSpec((8, D_MODEL), lambda l, j: (0, 0)),
            pl.BlockSpec((1, D_MODEL, n_tile), lambda l, j: (l, 0, j)),
            pl.BlockSpec((1, 1, n_tile), lambda l, j: (l, 0, j)),
        ],
        out_specs=pl.BlockSpec((1, 8, n_tile), lambda l, j: (l, 0, j)),
        out_shape=jax.ShapeDtypeStruct((DEPTH, 8, N_MOD * D_MODEL), F32),
        compiler_params=_params(("parallel", "parallel")),
        name="adaln_mods",
    )(cond8, w_mod, b_mod.reshape(DEPTH, 1, N_MOD * D_MODEL))


def _sub_tiles(tm, sub_rows=SUB_ROWS):
    return [slice(lo, lo + sub_rows) for lo in range(0, tm, sub_rows)]


def _ffn_kernel(x_ref, mods_ref, gpre_ref, gpost_ref, wgu_ref, wdn_ref, o_ref, *, s):
    rows = _sub_tiles(x_ref.shape[0])
    gate = mods_ref[3 * s + 2:3 * s + 3, :]

    def pre(r):
        return _pre(x_ref[rows[r], :], gpre_ref[...], mods_ref, s).astype(BF16)

    def post(r, f):
        o_ref[rows[r], :] = x_ref[rows[r], :] + (0.5 * gate) * (_rms(f) * gpost_ref[...])

    h = pre(0)
    f_prev = None
    for r in range(len(rows)):
        g = _dot(h, wgu_ref[:, :D_FF])
        u = _dot(h, wgu_ref[:, D_FF:])
        a = ((g * _sigmoid(g)) * u).astype(BF16)
        if r + 1 < len(rows):
            h = pre(r + 1)
        f = _dot(a, wdn_ref[...])
        if r > 0:
            post(r - 1, f_prev)
        f_prev = f
    post(len(rows) - 1, f_prev)


def _ffn_call(x, mods, cond, P, l, s):
    rows = x.shape[0]
    tm = TM_TOKEN
    j = s // 2
    return pl.pallas_call(
        functools.partial(_ffn_kernel, s=s),
        grid=(rows // tm,),
        in_specs=[
            pl.BlockSpec((tm, D_MODEL), lambda i: (i, 0)),
            _mods_spec(l, cond[0], rows // cond[1] // tm),
            _sel_spec(P["g_pre"], (l, s)),
            _sel_spec(P["g_post"], (l, s)),
            _sel_spec(P["w_ffn_gu"], (l, j)),
            _sel_spec(P["w_ffn_dn"], (l, j)),
        ],
        out_specs=pl.BlockSpec((tm, D_MODEL), lambda i: (i, 0)),
        out_shape=jax.ShapeDtypeStruct((rows, D_MODEL), F32),
        compiler_params=_params(("parallel",)),
        name="ffn_half_step",
    )(x, mods, P["g_pre"], P["g_post"], P["w_ffn_gu"], P["w_ffn_dn"])


def _front_kernel(*refs, latent):
    if latent:
        (x_ref, mods_ref, gpre_ref, win_ref, gq_ref, wq_ref, gkv_ref, wk_ref, wv_ref,
         cos_ref, sin_ref, cost_ref, sint_ref, zloc_ref, q_ref, k_ref, v_ref) = refs
    else:
        (x_ref, mods_ref, gpre_ref, win_ref, gq_ref, wq_ref, gkv_ref, wk_ref, wv_ref,
         zloc_ref, q_ref, k_ref, v_ref, ckv_ref, kr_ref) = refs
    q_scale = float(QK_NOPE + QK_ROPE) ** -0.5 * LOG2E
    tm = x_ref.shape[0]

    def project(rs, z):
        zloc_ref[rs, :] = z[:, :LOC_W]
        qn = (_rms(z[:, LOC_W:LOC_W + Q_LORA]) * gq_ref[...]).astype(BF16)
        ckv_n = _rms(z[:, LOC_W + Q_LORA:MAIN_W]) * gkv_ref[...]
        kr = z[:, MAIN_W:MAIN_W + HEAD_PAD]
        ckv_b = ckv_n.astype(BF16)
        kn = _dot(ckv_b, wk_ref[...])
        if latent:
            kr = kr * cos_ref[rs, :] + z[:, MAIN_W + HEAD_PAD:] * sin_ref[rs, :]
            vt = _dot_nt(wv_ref[...], ckv_b)
            qt = _dot_nt(wq_ref[0:QK_PAD, :], qn)
            qt_sw = _dot_nt(wq_ref[QK_PAD:, :], qn)
            cost = cost_ref[:, rs]
            sint = sint_ref[:, rs]
            ones = jnp.ones((V_ROWS - V_HEAD, tm), BF16)
            for hd in range(MLA_HEADS):
                sl = slice(hd * HEAD_PAD, (hd + 1) * HEAD_PAD)
                qh = ((qt[sl, :] * cost + qt_sw[sl, :] * sint) * q_scale).astype(BF16)
                for c in range(tm // TQ_ATTN):
                    q_ref[hd, c] = qh[:, c * TQ_ATTN:(c + 1) * TQ_ATTN]
                k_ref[hd, rs, :] = (kn[:, sl] + kr).astype(BF16)
                v_ref[hd, 0:V_HEAD, rs] = vt[hd * V_HEAD:(hd + 1) * V_HEAD, :].astype(BF16)
                v_ref[hd, V_HEAD:, rs] = ones
        else:
            ckv_ref[rs, :] = ckv_n
            kr_ref[rs, :] = kr[:, QK_NOPE:QK_NOPE + QK_ROPE]
            v_ref[rs, :] = _dot(ckv_b, wv_ref[...]).astype(BF16)
            q_ref[rs, :] = (_dot(qn, wq_ref[...]) * q_scale).astype(BF16)
            for hd in range(MLA_HEADS):
                sl = slice(hd * HEAD_PAD, (hd + 1) * HEAD_PAD)
                k_ref[rs, sl] = (kn[:, sl] + kr).astype(BF16)

    h = _pre(x_ref[...], gpre_ref[...], mods_ref, 1).astype(BF16)
    project(slice(0, tm), _dot(h, win_ref[...]))


def _front_call(x, mods, cond, P, l, rope_tabs):
    rows = x.shape[0]
    tm = TM_FRONT
    latent = rope_tabs is not None
    win = P["win_lat"] if latent else P["win_ctx"]
    wq = P["wqt_lat"] if latent else P["wq_ctx"]
    wv = P["wvt"] if latent else P["wv"]
    row_spec = lambda w: pl.BlockSpec((tm, w), lambda i: (i, 0))
    in_specs = [
        row_spec(D_MODEL),
        _mods_spec(l, cond[0], rows // cond[1] // tm),
        _sel_spec(P["g_pre"], (l, 1)),
        _sel_spec(win, (l,)),
        _sel_spec(P["g_q"], (l,)),
        _sel_spec(wq, (l,)),
        _sel_spec(P["g_kv"], (l,)),
        _sel_spec(P["wk"], (l,)),
        _sel_spec(wv, (l,)),
    ]
    args = [x, mods, P["g_pre"], win, P["g_q"], wq, P["g_kv"], P["wk"], wv]
    if latent:
        cos_tab, sin_tab, cos_tab_t, sin_tab_t = rope_tabs
        seq_tiles = cos_tab.shape[0] // tm
        in_specs += [pl.BlockSpec((tm, HEAD_PAD), lambda i: (i % seq_tiles, 0))] * 2
        in_specs += [pl.BlockSpec((HEAD_PAD, tm), lambda i: (0, i % seq_tiles))] * 2
        args += [cos_tab, sin_tab, cos_tab_t, sin_tab_t]
        out_specs = [
            row_spec(LOC_W),
            pl.BlockSpec((MLA_HEADS, tm // TQ_ATTN, HEAD_PAD, TQ_ATTN), lambda i: (0, i, 0, 0)),
            pl.BlockSpec((MLA_HEADS, tm, HEAD_PAD), lambda i: (0, i, 0)),
            pl.BlockSpec((MLA_HEADS, V_ROWS, tm), lambda i: (0, 0, i)),
        ]
        out_shape = [
            jax.ShapeDtypeStruct((rows, LOC_W), F32),
            jax.ShapeDtypeStruct((MLA_HEADS, rows // TQ_ATTN, HEAD_PAD, TQ_ATTN), BF16),
            jax.ShapeDtypeStruct((MLA_HEADS, rows, HEAD_PAD), BF16),
            jax.ShapeDtypeStruct((MLA_HEADS, V_ROWS, rows), BF16),
        ]
    else:
        out_specs = [row_spec(LOC_W), row_spec(QK_PAD), row_spec(QK_PAD), row_spec(MLA_W),
                     row_spec(KV_LORA), row_spec(QK_ROPE)]
        out_shape = [
            jax.ShapeDtypeStruct((rows, LOC_W), F32),
            jax.ShapeDtypeStruct((rows, QK_PAD), BF16),
            jax.ShapeDtypeStruct((rows, QK_PAD), BF16),
            jax.ShapeDtypeStruct((rows, MLA_W), BF16),
            jax.ShapeDtypeStruct((rows, KV_LORA), F32),
            jax.ShapeDtypeStruct((rows, QK_ROPE), F32),
        ]
    return pl.pallas_call(
        functools.partial(_front_kernel, latent=latent),
        grid=(rows // tm,),
        in_specs=in_specs,
        out_specs=out_specs,
        out_shape=out_shape,
        compiler_params=_params(("parallel",)),
        name="mix_front_latent" if latent else "mix_front_context",
    )(*args)


def _kvup_kernel(ckv_ref, kr_ref, wk_ref, wvt_ref, place_ref, k_ref, vt_ref):
    ckv_b = ckv_ref[...].astype(BF16)
    kn = _dot(ckv_b, wk_ref[...])
    kr = _dot(kr_ref[...].astype(BF16), place_ref[...])
    vt = _dot_nt(wvt_ref[...], ckv_b)
    ones = jnp.ones((V_ROWS - V_HEAD, ckv_b.shape[0]), BF16)
    for hd in range(MLA_HEADS):
        sl = slice(hd * HEAD_PAD, (hd + 1) * HEAD_PAD)
        k_ref[hd] = (kn[:, sl] + kr).astype(BF16)
        vt_ref[hd, 0:V_HEAD, :] = vt[hd * V_HEAD:(hd + 1) * V_HEAD, :].astype(BF16)
        vt_ref[hd, V_HEAD:, :] = ones


def _kvup_call(cache_ckv, cache_krope, P, l, place):
    b, _, n, _ = cache_ckv.shape
    return pl.pallas_call(
        _kvup_kernel,
        grid=(b,),
        in_specs=[
            pl.BlockSpec((None, None, n, KV_LORA), lambda i: (i, l, 0, 0)),
            pl.BlockSpec((None, None, n, QK_ROPE), lambda i: (i, l, 0, 0)),
            _sel_spec(P["wk"], (l,)),
            _sel_spec(P["wvt"], (l,)),
            _sel_spec(place),
        ],
        out_specs=[pl.BlockSpec((None, MLA_HEADS, n, HEAD_PAD), lambda i: (i, 0, 0, 0)),
                   pl.BlockSpec((None, MLA_HEADS, V_ROWS, n), lambda i: (i, 0, 0, 0))],
        out_shape=[jax.ShapeDtypeStruct((b, MLA_HEADS, n, HEAD_PAD), BF16),
                   jax.ShapeDtypeStruct((b, MLA_HEADS, V_ROWS, n), BF16)],
        compiler_params=_params(("parallel",)),
        name="cache_kv_up",
    )(cache_ckv, cache_krope, P["wk"], P["wvt"], place)


def _by_group(grp, r):
    return jnp.where(grp == 0, r[0:CHUNK],
                     jnp.where(grp == 1, r[CHUNK:2 * CHUNK],
                               jnp.where(grp == 2, r[2 * CHUNK:3 * CHUNK], r[3 * CHUNK:])))


def _local_kernel(z_ref, zprev_ref, znext_ref, band_ref, wpool_ref, pscale_ref, gsgu_ref, wsgu_ref,
                  bsgu_ref, convw_ref, o_ref, ext_ref, *, seqlen, tm):
    i = pl.program_id(0)
    starts_seq = (i * tm) % seqlen == 0
    ends_seq = ((i + 1) * tm) % seqlen == 0
    t = (lax.broadcasted_iota(jnp.int32, (tm, 1), 0) + i * tm) & (seqlen - 1)
    lane = lax.broadcasted_iota(jnp.int32, (1, POOL_W), 1)
    grp = lane // POOL_GW

    zp = z_ref[:, 0:256]
    zu = z_ref[:, 256:512]
    zv = z_ref[:, 512:768]
    zb = z_ref[:, 768:1024]
    prod = z_ref[:, 1024:1280] * z_ref[:, 1280:1536]

    ext_ref[0:HALO, 0:256] = jnp.where(starts_seq, 0.0, zprev_ref[:, 0:256])
    ext_ref[0:HALO, 256:512] = jnp.where(starts_seq, 0.0, zprev_ref[:, 1024:1280] * zprev_ref[:, 1280:1536])
    ext_ref[HALO:HALO + tm, 0:256] = zp
    ext_ref[HALO:HALO + tm, 256:512] = prod
    ext_ref[HALO + tm:2 * HALO + tm, 0:256] = jnp.where(ends_seq, 0.0, znext_ref[:, 0:256])
    ext_ref[HALO + tm:2 * HALO + tm, 256:512] = jnp.where(
        ends_seq, 0.0, znext_ref[:, 1024:1280] * znext_ref[:, 1280:1536])
    ext_ref[2 * HALO + tm:, :] = jnp.zeros((CHUNK - 2 * HALO, 512), F32)

    xe = ext_ref[:, 0:256]
    hi = xe.astype(BF16)
    lo = (xe - hi.astype(F32)).astype(BF16)
    band = band_ref[...]
    sums = []
    for n in range(tm // CHUNK):
        ws = slice(n * CHUNK, n * CHUNK + 2 * CHUNK)
        sums.append(_by_group(grp, _dot(band, hi[ws, :]) + _dot(band, lo[ws, :])))
    wsum = jnp.concatenate(sums, axis=0) if len(sums) > 1 else sums[0]
    half = jnp.where(grp == 0, POOL_HALF[0],
                     jnp.where(grp == 1, POOL_HALF[1],
                               jnp.where(grp == 2, POOL_HALF[2], POOL_HALF[3])))
    cnt = (jnp.minimum(t + half, seqlen) - jnp.maximum(t - half, 0)).astype(F32)
    dpool = (wsum / cnt - zp).astype(BF16)
    o_ref[:, 0:256] = (_dot(dpool, wpool_ref[...]) * pscale_ref[...]).astype(BF16)

    vc = (_rms(zv) * gsgu_ref[...]).astype(BF16)
    wsgu = wsgu_ref[...]
    bias = bsgu_ref[...]
    for n in range(tm // CHUNK):
        rs = slice(n * CHUNK, (n + 1) * CHUNK)
        mixed = _by_group(grp, _dot(wsgu, vc[rs, :]))
        o_ref[rs, 256:512] = (zu[rs, :] * (mixed + bias)).astype(BF16)

    pm1 = ext_ref[HALO - 1:HALO - 1 + tm, 256:512]
    pp1 = ext_ref[HALO + 1:HALO + 1 + tm, 256:512]
    conv = pm1 * convw_ref[0:1, :] + prod * convw_ref[1:2, :] + pp1 * convw_ref[2:3, :]
    o_ref[:, 512:768] = (zb * conv).astype(BF16)


def _pool_band():
    p = np.arange(CHUNK)[:, None]
    j = np.arange(2 * CHUNK)[None, :]
    bands = [((j >= p + HALO - hw) & (j < p + HALO + hw)) for hw in POOL_HALF]
    return jnp.asarray(np.concatenate(bands, axis=0).astype(np.float32), dtype=BF16)


def _local_call(zloc, seqlen, P, l):
    rows = zloc.shape[0]
    tm = min(TM_LOCAL, seqlen)
    assert seqlen % tm == 0 and tm % CHUNK == 0
    hb = tm // HALO
    last = rows // HALO - 1
    names = ("wpool", "pool_scale", "g_sgu", "wsgu", "bsgu", "conv_w")
    return pl.pallas_call(
        functools.partial(_local_kernel, seqlen=seqlen, tm=tm),
        grid=(rows // tm,),
        in_specs=[
            pl.BlockSpec((tm, LOC_W), lambda i: (i, 0)),
            pl.BlockSpec((HALO, LOC_W), lambda i: (jnp.maximum(i * hb - 1, 0), 0)),
            pl.BlockSpec((HALO, LOC_W), lambda i: (jnp.minimum((i + 1) * hb, last), 0)),
            _sel_spec(P["band"]),
        ] + [_sel_spec(P[n], (l,)) for n in names],
        out_specs=pl.BlockSpec((tm, 768), lambda i: (i, 0)),
        out_shape=jax.ShapeDtypeStruct((rows, 768), BF16),
        scratch_shapes=[pltpu.VMEM((tm + CHUNK, 512), F32)],
        compiler_params=_params(("parallel",)),
        name="local_mixers",
    )(zloc, zloc, zloc, P["band"], *[P[n] for n in names])


def _attn_kernel(*refs, n_seg):
    qt_ref, qtn_ref = refs[0:2]
    k_refs = refs[2:2 + n_seg]
    vt_refs = refs[2 + n_seg:2 + 2 * n_seg]
    o_ref, sa_ref, sb_ref, ma_ref, mb_ref, ot_ref = refs[2 + 2 * n_seg:]
    buf_a = (sa_ref, ma_ref)
    buf_b = (sb_ref, mb_ref)
    bounds = [0]
    for k_ref in k_refs:
        bounds.append(bounds[-1] + k_ref.shape[1])

    def scores(q_ref, h, buf):
        s_ref, m_ref = buf
        qt = q_ref[h]
        m = None
        for k_ref, lo, hi in zip(k_refs, bounds[:-1], bounds[1:]):
            s = _dot(k_ref[h], qt)
            s_ref[lo:hi, :] = s
            ms = jnp.max(s, axis=0, keepdims=True)
            m = ms if m is None else jnp.maximum(m, ms)
        m_ref[...] = m

    def values(h, buf):
        s_ref, m_ref = buf
        m = m_ref[...]
        ot = None
        for vt_ref, lo, hi in zip(vt_refs, bounds[:-1], bounds[1:]):
            p = jnp.exp2(s_ref[lo:hi, :] - m)
            os_ = _dot(vt_ref[h], p.astype(BF16))
            ot = os_ if ot is None else ot + os_
        ot_ref[pl.ds(pl.multiple_of(h * V_HEAD, V_HEAD), V_HEAD), :] = ot[0:V_HEAD] / ot[V_HEAD:V_HEAD + 1]

    @pl.when(pl.program_id(1) == 0)
    def _():
        scores(qt_ref, 0, buf_a)

    def body(h, carry):
        @pl.when(h % 2 == 0)
        def _():
            scores(qt_ref, h + 1, buf_b)
            values(h, buf_a)

        @pl.when(h % 2 == 1)
        def _():
            scores(qt_ref, h + 1, buf_a)
            values(h, buf_b)
        return carry

    lax.fori_loop(0, MLA_HEADS - 1, body, 0)
    scores(qtn_ref, 0, buf_a)
    values(MLA_HEADS - 1, buf_b)
    o_ref[...] = ot_ref[...].T.astype(BF16)


def _attn_call(qt, segs, batch):
    tq = TQ_ATTN
    rows = qt.shape[1] * tq
    lq = rows // batch
    nq = lq // tq
    n_seg = len(segs)
    k_specs, vt_specs, k_args, vt_args, keys = [], [], [], [], 0
    for k, vt in segs:
        if k.ndim == 4:
            n = k.shape[2]
            k_specs.append(pl.BlockSpec((None, MLA_HEADS, n, HEAD_PAD), lambda i, j: (i, 0, 0, 0)))
            vt_specs.append(pl.BlockSpec((None, MLA_HEADS, V_ROWS, n), lambda i, j: (i, 0, 0, 0)))
        else:
            n = k.shape[1] // batch
            k_specs.append(pl.BlockSpec((MLA_HEADS, n, HEAD_PAD), lambda i, j: (0, i, 0)))
            vt_specs.append(pl.BlockSpec((MLA_HEADS, V_ROWS, n), lambda i, j: (0, 0, i)))
        k_args.append(k)
        vt_args.append(vt)
        keys += n
    assert MLA_HEADS % 2 == 0
    return pl.pallas_call(
        functools.partial(_attn_kernel, n_seg=n_seg),
        grid=(batch, nq),
        in_specs=[
            pl.BlockSpec((MLA_HEADS, None, HEAD_PAD, tq), lambda i, j: (0, i * nq + j, 0, 0)),
            pl.BlockSpec((MLA_HEADS, None, HEAD_PAD, tq),
                         lambda i, j: (0, i * nq + jnp.minimum(j + 1, nq - 1), 0, 0)),
        ] + k_specs + vt_specs,
        out_specs=pl.BlockSpec((tq, MLA_W), lambda i, j: (i * nq + j, 0)),
        out_shape=jax.ShapeDtypeStruct((rows, MLA_W), BF16),
        scratch_shapes=[pltpu.VMEM((keys, tq), F32), pltpu.VMEM((keys, tq), F32),
                        pltpu.VMEM((1, tq), F32), pltpu.VMEM((1, tq), F32),
                        pltpu.VMEM((MLA_W, tq), F32)],
        compiler_params=_params(("arbitrary", "arbitrary")),
        name="mla_attention",
    )(qt, qt, *k_args, *vt_args)


def _attn_short_kernel(q_ref, k_ref, v_ref, o_ref):
    lane = lax.broadcasted_iota(jnp.int32, (1, HEAD_PAD), 1)
    for j in range(MLA_HEADS // 2):
        vs = v_ref[:, j * HEAD_PAD:(j + 1) * HEAD_PAD]
        outs = []
        for hd in (2 * j, 2 * j + 1):
            sl = slice(hd * HEAD_PAD, (hd + 1) * HEAD_PAD)
            s = _dot_nt(q_ref[:, sl], k_ref[:, sl])
            p = jnp.exp2(s - jnp.max(s, axis=-1, keepdims=True))
            l = jnp.sum(p, axis=-1, keepdims=True)
            outs.append(_dot(p.astype(BF16), vs) / l)
        o_ref[:, j * HEAD_PAD:(j + 1) * HEAD_PAD] = jnp.where(lane < V_HEAD, outs[0], outs[1]).astype(BF16)


def _attn_short_call(q, k, v, seq):
    rows = q.shape[0]
    return pl.pallas_call(
        _attn_short_kernel,
        grid=(rows // seq,),
        in_specs=[
            pl.BlockSpec((seq, QK_PAD), lambda i: (i, 0)),
            pl.BlockSpec((seq, QK_PAD), lambda i: (i, 0)),
            pl.BlockSpec((seq, MLA_W), lambda i: (i, 0)),
        ],
        out_specs=pl.BlockSpec((seq, MLA_W), lambda i: (i, 0)),
        out_shape=jax.ShapeDtypeStruct((rows, MLA_W), BF16),
        compiler_params=_params(("parallel",)),
        name="mla_attention_context",
    )(q, k, v)


def _merge_kernel(x_ref, mods_ref, gpre_ref, gpost_ref, loc_ref, att_ref, wgate_ref, bgate_ref,
                  wbp_ref, wbs_ref, wbm_ref, wbc_ref, wo_ref, o_ref):
    rows = _sub_tiles(x_ref.shape[0])

    def pre(r):
        return _pre(x_ref[rows[r], :], gpre_ref[...], mods_ref, 1).astype(BF16)

    def post(r, m):
        o_ref[rows[r], :] = x_ref[rows[r], :] + mods_ref[5:6, :] * (_rms(m) * gpost_ref[...])

    h = pre(0)
    m_prev = None
    for r in range(len(rows)):
        rs = rows[r]
        branches = (
            (loc_ref[rs, 0:256], wbp_ref),
            (loc_ref[rs, 256:512], wbs_ref),
            (att_ref[rs, :], wbm_ref),
            (loc_ref[rs, 512:768], wbc_ref),
        )
        merged = None
        for j, (inp, w_ref) in enumerate(branches):
            sl = slice(j * D_MODEL, (j + 1) * D_MODEL)
            gate = _sigmoid(_dot(h, wgate_ref[:, sl]) + bgate_ref[:, sl])
            term = gate * _dot(inp, w_ref[...])
            merged = term if merged is None else merged + term
        if r + 1 < len(rows):
            h = pre(r + 1)
        m = _dot(merged.astype(BF16), wo_ref[...])
        if r > 0:
            post(r - 1, m_prev)
        m_prev = m
    post(len(rows) - 1, m_prev)


def _merge_call(x, mods, cond, P, l, loc, att):
    rows = x.shape[0]
    tm = TM_TOKEN
    row_spec = lambda w: pl.BlockSpec((tm, w), lambda i: (i, 0))
    names = ("w_gate", "b_gate", "w_br_pool", "w_br_sgu", "w_br_mla", "w_br_conv", "w_o")
    return pl.pallas_call(
        _merge_kernel,
        grid=(rows // tm,),
        in_specs=[
            row_spec(D_MODEL),
            _mods_spec(l, cond[0], rows // cond[1] // tm),
            _sel_spec(P["g_pre"], (l, 1)),
            _sel_spec(P["g_post"], (l, 1)),
            row_spec(768),
            row_spec(MLA_W),
        ] + [_sel_spec(P[n], (l,)) for n in names],
        out_specs=row_spec(D_MODEL),
        out_shape=jax.ShapeDtypeStruct((rows, D_MODEL), F32),
        compiler_params=_params(("parallel",)),
        name="gated_merge",
    )(x, mods, P["g_pre"], P["g_post"], loc, att, *[P[n] for n in names])


def _rope_swap_perm():
    j = np.arange(QK_ROPE)
    return np.where((j % 16) < 8, j + 8, j - 8)


def _prepare(w_in, w_uq, w_ukv, w_pool, w_sgu, b_sgu):
    perm = _rope_swap_perm()
    zp, zu, zv, cq, ckv, kr, zb, zc, zx = jnp.split(
        w_in, np.cumsum((256, 256, 256, 256, 128, 32, 256, 256)), axis=2)
    pad_rope = lambda a: jnp.pad(a, ((0, 0), (0, 0), ROPE_PAD))
    main = [zp, zu, zv, zb, zc, zx, cq, ckv, pad_rope(kr)]
    win_ctx = jnp.concatenate(main, axis=2).astype(BF16)
    win_lat = jnp.concatenate(main + [pad_rope(kr[:, :, perm])], axis=2).astype(BF16)

    uq = w_uq.reshape(DEPTH, Q_LORA, MLA_HEADS, QK_NOPE + QK_ROPE)
    wq = jnp.pad(uq, ((0, 0), (0, 0), (0, 0), (0, HEAD_PAD - QK_NOPE - QK_ROPE)))
    wq_sw = jnp.pad(uq[:, :, :, QK_NOPE:][:, :, :, perm], ((0, 0), (0, 0), (0, 0), ROPE_PAD))
    to_t = lambda a: a.reshape(DEPTH, Q_LORA, QK_PAD).transpose(0, 2, 1)
    wq_ctx = wq.reshape(DEPTH, Q_LORA, QK_PAD).astype(BF16)
    wqt_lat = jnp.concatenate([to_t(wq), to_t(wq_sw)], axis=1).astype(BF16)

    ukv = w_ukv.reshape(DEPTH, KV_LORA, MLA_HEADS, QK_NOPE + V_HEAD)
    wk = jnp.pad(ukv[:, :, :, :QK_NOPE], ((0, 0), (0, 0), (0, 0), (0, HEAD_PAD - QK_NOPE)))
    wk = wk.reshape(DEPTH, KV_LORA, QK_PAD).astype(BF16)
    wv = ukv[:, :, :, QK_NOPE:].reshape(DEPTH, KV_LORA, MLA_W).astype(BF16)
    wvt = wv.transpose(0, 2, 1)

    eye = jnp.eye(POOL_GROUPS, dtype=F32)
    wpool = jnp.einsum("lgcd,gh->lgchd", w_pool, eye).reshape(DEPTH, POOL_W, POOL_W).astype(BF16)
    wsgu = w_sgu.reshape(DEPTH, SGU_GROUPS * CHUNK, CHUNK).astype(BF16)
    bsgu = jnp.repeat(b_sgu.transpose(0, 2, 1), SGU_W // SGU_GROUPS, axis=2)
    return dict(win_ctx=win_ctx, win_lat=win_lat, wq_ctx=wq_ctx, wqt_lat=wqt_lat, wk=wk, wv=wv, wvt=wvt,
                wpool=wpool, wsgu=wsgu, bsgu=bsgu, band=_pool_band())


def _rope_tables(n):
    rows = n // GRID_W
    r = jnp.broadcast_to(jnp.arange(rows)[:, None], (rows, GRID_W)).reshape(-1).astype(F32)
    col = jnp.broadcast_to(jnp.arange(GRID_W)[None, :], (rows, GRID_W)).reshape(-1).astype(F32)
    half = QK_ROPE // 2
    freqs = ROPE_THETA ** (-(2.0 * jnp.arange(half // 2, dtype=F32)) / half)
    ang = jnp.stack([r[:, None] * freqs, col[:, None] * freqs], axis=1)
    cos, sin = jnp.cos(ang), jnp.sin(ang)
    cos32 = jnp.concatenate([cos, cos], axis=-1).reshape(n, QK_ROPE)
    sin32 = jnp.concatenate([-sin, sin], axis=-1).reshape(n, QK_ROPE)
    cos_tab = jnp.concatenate([jnp.ones((n, QK_NOPE), F32), cos32,
                               jnp.zeros((n, HEAD_PAD - QK_NOPE - QK_ROPE), F32)], axis=1)
    sin_tab = jnp.pad(sin32, ((0, 0), ROPE_PAD))
    return cos_tab, sin_tab, cos_tab.T, sin_tab.T


def kernel(x_prompt, x_sample, cache_ckv, cache_krope, c, c_ctx, w_mod, b_mod, g_pre, g_post,
           w_ffn_gu, w_ffn_dn, w_in, w_pool, pool_scale, g_sgu, w_sgu, b_sgu, g_q, w_uq, g_kv,
           w_ukv, conv_w, w_br_pool, w_br_sgu, w_br_mla, w_br_conv, w_gate, b_gate, w_o):
    batch, seq, _ = x_prompt.shape
    dec_batch, dec_seq, _ = x_sample.shape

    cond8 = jnp.concatenate([c_ctx[None, :], c, jnp.zeros((8 - 1 - dec_batch, D_MODEL), F32)], axis=0)
    mods = _mods_call(cond8, w_mod, b_mod).reshape(DEPTH, 8, N_MOD, D_MODEL)
    rope_tabs = _rope_tables(dec_seq)
    place = jnp.pad(jnp.eye(QK_ROPE, dtype=F32), ((0, 0), ROPE_PAD)).astype(BF16)

    P = _prepare(w_in, w_uq, w_ukv, w_pool, w_sgu, b_sgu)
    row = lambda a: a.reshape(a.shape[:-1] + (1, a.shape[-1]))
    P.update(
        g_pre=row(g_pre), g_post=row(g_post), g_q=row(g_q), g_kv=row(g_kv), g_sgu=row(g_sgu),
        pool_scale=row(pool_scale), b_gate=row(b_gate), conv_w=conv_w,
        w_ffn_gu=w_ffn_gu.astype(BF16), w_ffn_dn=w_ffn_dn.astype(BF16), w_gate=w_gate.astype(BF16),
        w_br_pool=w_br_pool.astype(BF16), w_br_sgu=w_br_sgu.astype(BF16),
        w_br_mla=w_br_mla.astype(BF16), w_br_conv=w_br_conv.astype(BF16), w_o=w_o.astype(BF16),
    )

    streams = {"ctx": (x_prompt.reshape(batch * seq, D_MODEL), (0, 1), batch, seq),
               "lat": (x_sample.reshape(dec_batch * dec_seq, D_MODEL), (1, dec_batch), dec_batch, dec_seq)}
    outs = {}
    ckv_states, kr_states = [], []
    for name, (x, cond, nb, ns) in streams.items():
        is_ctx = name == "ctx"
        for l in range(DEPTH):
            x = _ffn_call(x, mods, cond, P, l, 0)
            front = _front_call(x, mods, cond, P, l, None if is_ctx else rope_tabs)
            zloc, q, k, v = front[:4]
            loc = _local_call(zloc, ns, P, l)
            if is_ctx:
                ckv_states.append(front[4].reshape(batch, seq, KV_LORA))
                kr_states.append(front[5].reshape(batch, seq, QK_ROPE))
                att = _attn_short_call(q, k, v, ns)
            else:
                att = _attn_call(q, [_kvup_call(cache_ckv, cache_krope, P, l, place), (k, v)], nb)
            x = _merge_call(x, mods, cond, P, l, loc, att)
            x = _ffn_call(x, mods, cond, P, l, 2)
        outs[name] = x

    y_prompt = outs["ctx"].reshape(batch, seq, D_MODEL)
    y_sample = outs["lat"].reshape(dec_batch, dec_seq, D_MODEL)
    state_ckv = jnp.stack(ckv_states, axis=1)
    state_krope = jnp.stack(kr_states, axis=1)
    return (y_prompt, y_sample, state_ckv, state_krope)
```

```python
import functools

import numpy as np
import jax
import jax.numpy as jnp
from jax import lax
from jax.experimental import pallas as pl
from jax.experimental.pallas import tpu as pltpu

F32 = jnp.float32
BF16 = jnp.bfloat16

D_MODEL = 1024
DEPTH = 2
GRID_W = 64
N_MOD = 9
D_FF = 2816
EPS = 1e-6
POOL_GROUPS = 4
POOL_GW = 64
POOL_W = 256
SGU_GROUPS = 4
SGU_W = 256
CHUNK = 128
MLA_HEADS = 8
QK_NOPE = 64
QK_ROPE = 32
V_HEAD = 64
Q_LORA = 256
KV_LORA = 128
MLA_W = MLA_HEADS * V_HEAD
ROPE_THETA = 10000.0
CONV_W = 256
CONV_K = 3
N_BRANCH = 4
HEAD_PAD = 128
QK_PAD = MLA_HEADS * HEAD_PAD
ROPE_PAD = (QK_NOPE, HEAD_PAD - QK_NOPE - QK_ROPE)
LOC_W = 6 * 256
MAIN_W = LOC_W + Q_LORA + KV_LORA
HALO = 8
POOL_HALF = (1, 2, 4, 8)
LOG2E = 1.4426950408889634

VMEM_LIMIT = 56 * 1024 * 1024

TM_TOKEN = 1024
TM_MERGE = 512
SUB_ROWS = 256
TM_FRONT = 512
V_ROWS = V_HEAD + 16
TQ_ATTN = 256


def _sel_spec(arr, idx=(), cols=None):
    rest = arr.shape[len(idx):]
    if cols is not None:
        rest = rest[:-1] + (cols,)
    zeros = (0,) * len(rest)
    return pl.BlockSpec((None,) * len(idx) + rest, lambda *_: tuple(idx) + zeros,
                        pipeline_mode=pl.Buffered(1))


def _mods_spec(l, cond0, per_cond):
    return pl.BlockSpec((None, None, N_MOD, D_MODEL), lambda i: (l, cond0 + i // per_cond, 0, 0))


def _params(sem):
    return pltpu.CompilerParams(dimension_semantics=sem, vmem_limit_bytes=VMEM_LIMIT)


def _sigmoid(x):
    return 1.0 / (1.0 + jnp.exp(-x))


def _rms(x):
    return x * lax.rsqrt(jnp.mean(x * x, axis=-1, keepdims=True) + EPS)


def _pre(x, g, mods_ref, s):
    return (_rms(x) * g) * (1.0 + mods_ref[3 * s + 1:3 * s + 2, :]) + mods_ref[3 * s:3 * s + 1, :]


def _dot(a, b):
    return jnp.dot(a, b, preferred_element_type=F32)


def _dot_nt(a, b):
    return lax.dot_general(a, b, (((1,), (1,)), ((), ())), preferred_element_type=F32)


def _mods_kernel(cond_ref, w_ref, b_ref, o_ref):
    c = cond_ref[...]
    a = (c * _sigmoid(c)).astype(BF16)
    o_ref[0] = _dot(a, w_ref[0].astype(BF16)) + b_ref[0]


def _mods_call(cond8, w_mod, b_mod):
    n_tile = 1024
    nt = (N_MOD * D_MODEL) // n_tile
    return pl.pallas_call(
        _mods_kernel,
        grid=(DEPTH, nt),
        in_specs=[
            pl.BlockSpec((8, D_MODEL), lambda l, j: (0, 0)),
            pl.BlockSpec((1, D_MODEL, n_tile), lambda l, j: (l, 0, j)),
            pl.BlockSpec((1, 1, n_tile), lambda l, j: (l, 0, j)),
        ],
        out_specs=pl.BlockSpec((1, 8, n_tile), lambda l, j: (l, 0, j)),
        out_shape=jax.ShapeDtypeStruct((DEPTH, 8, N_MOD * D_MODEL), F32),
        compiler_params=_params(("parallel", "parallel")),
        name="adaln_mods",
    )(cond8, w_mod, b_mod.reshape(DEPTH, 1, N_MOD * D_MODEL))


def _sub_tiles(tm, sub_rows=SUB_ROWS):
    return [slice(lo, lo + sub_rows) for lo in range(0, tm, sub_rows)]


def _ffn_kernel(x_ref, mods_ref, gpre_ref, gpost_ref, wgu_ref, wdn_ref, o_ref, *, s):
    rows = _sub_tiles(x_ref.shape[0])
    gate = mods_ref[3 * s + 2:3 * s + 3, :]

    def pre(r):
        return _pre(x_ref[rows[r], :], gpre_ref[...], mods_ref, s).astype(BF16)

    def post(r, f):
        o_ref[rows[r], :] = x_ref[rows[r], :] + (0.5 * gate) * (_rms(f) * gpost_ref[...])

    h = pre(0)
    f_prev = None
    for r in range(len(rows)):
        g = _dot(h, wgu_ref[:, :D_FF])
        u = _dot(h, wgu_ref[:, D_FF:])
        a = ((g * _sigmoid(g)) * u).astype(BF16)
        if r + 1 < len(rows):
            h = pre(r + 1)
        f = _dot(a, wdn_ref[...])
        if r > 0:
            post(r - 1, f_prev)
        f_prev = f
    post(len(rows) - 1, f_prev)


def _ffn_call(x, mods, cond, P, l, s):
    rows = x.shape[0]
    tm = TM_TOKEN
    j = s // 2
    return pl.pallas_call(
        functools.partial(_ffn_kernel, s=s),
        grid=(rows // tm,),
        in_specs=[
            pl.BlockSpec((tm, D_MODEL), lambda i: (i, 0)),
            _mods_spec(l, cond[0], rows // cond[1] // tm),
            _sel_spec(P["g_pre"], (l, s)),
            _sel_spec(P["g_post"], (l, s)),
            _sel_spec(P["w_ffn_gu"], (l, j)),
            _sel_spec(P["w_ffn_dn"], (l, j)),
        ],
        out_specs=pl.BlockSpec((tm, D_MODEL), lambda i: (i, 0)),
        out_shape=jax.ShapeDtypeStruct((rows, D_MODEL), F32),
        compiler_params=_params(("parallel",)),
        name="ffn_half_step",
    )(x, mods, P["g_pre"], P["g_post"], P["w_ffn_gu"], P["w_ffn_dn"])


def _front_kernel(*refs, latent):
    if latent:
        (x_ref, mods_ref, gpre_ref, win_ref, gq_ref, wq_ref, gkv_ref, wk_ref, wv_ref,
         cos_ref, sin_ref, cost_ref, sint_ref, zloc_ref, q_ref, k_ref, v_ref) = refs
    else:
        (x_ref, mods_ref, gpre_ref, win_ref, gq_ref, wq_ref, gkv_ref, wk_ref, wv_ref,
         zloc_ref, q_ref, k_ref, v_ref, ckv_ref, kr_ref) = refs
    q_scale = float(QK_NOPE + QK_ROPE) ** -0.5 * LOG2E
    tm = x_ref.shape[0]

    def project(rs, z):
        zloc_ref[rs, :] = z[:, :LOC_W]
        qn = (_rms(z[:, LOC_W:LOC_W + Q_LORA]) * gq_ref[...]).astype(BF16)
        ckv_n = _rms(z[:, LOC_W + Q_LORA:MAIN_W]) * gkv_ref[...]
        kr = z[:, MAIN_W:MAIN_W + HEAD_PAD]
        ckv_b = ckv_n.astype(BF16)
        kn = _dot(ckv_b, wk_ref[...])
        if latent:
            kr = kr * cos_ref[rs, :] + z[:, MAIN_W + HEAD_PAD:] * sin_ref[rs, :]
            vt = _dot_nt(wv_ref[...], ckv_b)
            qt = _dot_nt(wq_ref[0:QK_PAD, :], qn)
            qt_sw = _dot_nt(wq_ref[QK_PAD:, :], qn)
            cost = cost_ref[:, rs]
            sint = sint_ref[:, rs]
            ones = jnp.ones((V_ROWS - V_HEAD, tm), BF16)
            for hd in range(MLA_HEADS):
                sl = slice(hd * HEAD_PAD, (hd + 1) * HEAD_PAD)
                qh = ((qt[sl, :] * cost + qt_sw[sl, :] * sint) * q_scale).astype(BF16)
                for c in range(tm // TQ_ATTN):
                    q_ref[hd, c] = qh[:, c * TQ_ATTN:(c + 1) * TQ_ATTN]
                k_ref[hd, rs, :] = (kn[:, sl] + kr).astype(BF16)
                v_ref[hd, 0:V_HEAD, rs] = vt[hd * V_HEAD:(hd + 1) * V_HEAD, :].astype(BF16)
                v_ref[hd, V_HEAD:, rs] = ones
        else:
            ckv_ref[rs, :] = ckv_n
            kr_ref[rs, :] = kr[:, QK_NOPE:QK_NOPE + QK_ROPE]
            v_ref[rs, :] = _dot(ckv_b, wv_ref[...]).astype(BF16)
            q_ref[rs, :] = (_dot(qn, wq_ref[...]) * q_scale).astype(BF16)
            for hd in range(MLA_HEADS):
                sl = slice(hd * HEAD_PAD, (hd + 1) * HEAD_PAD)
                k_ref[rs, sl] = (kn[:, sl] + kr).astype(BF16)

    h = _pre(x_ref[...], gpre_ref[...], mods_ref, 1).astype(BF16)
    project(slice(0, tm), _dot(h, win_ref[...]))


def _front_call(x, mods, cond, P, l, rope_tabs):
    rows = x.shape[0]
    tm = TM_FRONT
    latent = rope_tabs is not None
    wq = P["wqt"] if latent else P["wq"]
    wv = P["wvt"] if latent else P["wv"]
    row_spec = lambda w: pl.BlockSpec((tm, w), lambda i: (i, 0))
    in_specs = [
        row_spec(D_MODEL),
        _mods_spec(l, cond[0], rows // cond[1] // tm),
        _sel_spec(P["g_pre"], (l, 1)),
        _sel_spec(P["win"], (l,), None if latent else MAIN_W + HEAD_PAD),
        _sel_spec(P["g_q"], (l,)),
        _sel_spec(wq, (l,), None if latent else QK_PAD),
        _sel_spec(P["g_kv"], (l,)),
        _sel_spec(P["wk"], (l,)),
        _sel_spec(wv, (l,)),
    ]
    args = [x, mods, P["g_pre"], P["win"], P["g_q"], wq, P["g_kv"], P["wk"], wv]
    if latent:
        cos_tab, sin_tab, cos_tab_t, sin_tab_t = rope_tabs
        seq_tiles = cos_tab.shape[0] // tm
        in_specs += [pl.BlockSpec((tm, HEAD_PAD), lambda i: (i % seq_tiles, 0))] * 2
        in_specs += [pl.BlockSpec((HEAD_PAD, tm), lambda i: (0, i % seq_tiles))] * 2
        args += [cos_tab, sin_tab, cos_tab_t, sin_tab_t]
        out_specs = [
            row_spec(LOC_W),
            pl.BlockSpec((MLA_HEADS, tm // TQ_ATTN, HEAD_PAD, TQ_ATTN), lambda i: (0, i, 0, 0)),
            pl.BlockSpec((MLA_HEADS, tm, HEAD_PAD), lambda i: (0, i, 0)),
            pl.BlockSpec((MLA_HEADS, V_ROWS, tm), lambda i: (0, 0, i)),
        ]
        out_shape = [
            jax.ShapeDtypeStruct((rows, LOC_W), F32),
            jax.ShapeDtypeStruct((MLA_HEADS, rows // TQ_ATTN, HEAD_PAD, TQ_ATTN), BF16),
            jax.ShapeDtypeStruct((MLA_HEADS, rows, HEAD_PAD), BF16),
            jax.ShapeDtypeStruct((MLA_HEADS, V_ROWS, rows), BF16),
        ]
    else:
        out_specs = [row_spec(LOC_W), row_spec(QK_PAD), row_spec(QK_PAD), row_spec(MLA_W),
                     row_spec(KV_LORA), row_spec(QK_ROPE)]
        out_shape = [
            jax.ShapeDtypeStruct((rows, LOC_W), F32),
            jax.ShapeDtypeStruct((rows, QK_PAD), BF16),
            jax.ShapeDtypeStruct((rows, QK_PAD), BF16),
            jax.ShapeDtypeStruct((rows, MLA_W), BF16),
            jax.ShapeDtypeStruct((rows, KV_LORA), F32),
            jax.ShapeDtypeStruct((rows, QK_ROPE), F32),
        ]
    return pl.pallas_call(
        functools.partial(_front_kernel, latent=latent),
        grid=(rows // tm,),
        in_specs=in_specs,
        out_specs=out_specs,
        out_shape=out_shape,
        compiler_params=_params(("parallel",)),
        name="mix_front_latent" if latent else "mix_front_context",
    )(*args)


def _kvup_kernel(ckv_ref, kr_ref, wk_ref, wvt_ref, place_ref, k_ref, vt_ref):
    ckv_b = ckv_ref[...].astype(BF16)
    kn = _dot(ckv_b, wk_ref[...])
    kr = _dot(kr_ref[...].astype(BF16), place_ref[...])
    vt = _dot_nt(wvt_ref[...], ckv_b)
    ones = jnp.ones((V_ROWS - V_HEAD, ckv_b.shape[0]), BF16)
    for hd in range(MLA_HEADS):
        sl = slice(hd * HEAD_PAD, (hd + 1) * HEAD_PAD)
        k_ref[hd] = (kn[:, sl] + kr).astype(BF16)
        vt_ref[hd, 0:V_HEAD, :] = vt[hd * V_HEAD:(hd + 1) * V_HEAD, :].astype(BF16)
        vt_ref[hd, V_HEAD:, :] = ones


def _kvup_call(cache_ckv, cache_krope, P, l, place):
    b, _, n, _ = cache_ckv.shape
    return pl.pallas_call(
        _kvup_kernel,
        grid=(b,),
        in_specs=[
            pl.BlockSpec((None, None, n, KV_LORA), lambda i: (i, l, 0, 0)),
            pl.BlockSpec((None, None, n, QK_ROPE), lambda i: (i, l, 0, 0)),
            _sel_spec(P["wk"], (l,)),
            _sel_spec(P["wvt"], (l,)),
            _sel_spec(place),
        ],
        out_specs=[pl.BlockSpec((None, MLA_HEADS, n, HEAD_PAD), lambda i: (i, 0, 0, 0)),
                   pl.BlockSpec((None, MLA_HEADS, V_ROWS, n), lambda i: (i, 0, 0, 0))],
        out_shape=[jax.ShapeDtypeStruct((b, MLA_HEADS, n, HEAD_PAD), BF16),
                   jax.ShapeDtypeStruct((b, MLA_HEADS, V_ROWS, n), BF16)],
        compiler_params=_params(("parallel",)),
        name="cache_kv_up",
    )(cache_ckv, cache_krope, P["wk"], P["wvt"], place)


def _by_group(grp, r):
    return jnp.where(grp == 0, r[0:CHUNK],
                     jnp.where(grp == 1, r[CHUNK:2 * CHUNK],
                               jnp.where(grp == 2, r[2 * CHUNK:3 * CHUNK], r[3 * CHUNK:])))


def _pool_band():
    p = np.arange(CHUNK)[:, None]
    j = np.arange(2 * CHUNK)[None, :]
    bands = [((j >= p + HALO - hw) & (j < p + HALO + hw)) for hw in POOL_HALF]
    return jnp.asarray(np.concatenate(bands, axis=0).astype(np.float32), dtype=BF16)


def _attn_kernel(*refs, n_seg):
    qt_ref, qtn_ref = refs[0:2]
    k_refs = refs[2:2 + n_seg]
    vt_refs = refs[2 + n_seg:2 + 2 * n_seg]
    o_ref, sa_ref, sb_ref, ma_ref, mb_ref, ot_ref = refs[2 + 2 * n_seg:]
    buf_a = (sa_ref, ma_ref)
    buf_b = (sb_ref, mb_ref)
    bounds = [0]
    for k_ref in k_refs:
        bounds.append(bounds[-1] + k_ref.shape[1])

    def scores(q_ref, h, buf):
        s_ref, m_ref = buf
        qt = q_ref[h]
        m = None
        for k_ref, lo, hi in zip(k_refs, bounds[:-1], bounds[1:]):
            s = _dot(k_ref[h], qt)
            s_ref[lo:hi, :] = s
            ms = jnp.max(s, axis=0, keepdims=True)
            m = ms if m is None else jnp.maximum(m, ms)
        m_ref[...] = m

    def values(h, buf):
        s_ref, m_ref = buf
        m = m_ref[...]
        ot = None
        for vt_ref, lo, hi in zip(vt_refs, bounds[:-1], bounds[1:]):
            p = jnp.exp2(s_ref[lo:hi, :] - m)
            os_ = _dot(vt_ref[h], p.astype(BF16))
            ot = os_ if ot is None else ot + os_
        ot_ref[pl.ds(pl.multiple_of(h * V_HEAD, V_HEAD), V_HEAD), :] = ot[0:V_HEAD] / ot[V_HEAD:V_HEAD + 1]

    @pl.when(pl.program_id(1) == 0)
    def _():
        scores(qt_ref, 0, buf_a)

    def body(h, carry):
        @pl.when(h % 2 == 0)
        def _():
            scores(qt_ref, h + 1, buf_b)
            values(h, buf_a)

        @pl.when(h % 2 == 1)
        def _():
            scores(qt_ref, h + 1, buf_a)
            values(h, buf_b)
        return carry

    lax.fori_loop(0, MLA_HEADS - 1, body, 0)
    scores(qtn_ref, 0, buf_a)
    values(MLA_HEADS - 1, buf_b)
    o_ref[...] = ot_ref[...].T.astype(BF16)


def _attn_call(qt, segs, batch):
    tq = TQ_ATTN
    rows = qt.shape[1] * tq
    lq = rows // batch
    nq = lq // tq
    n_seg = len(segs)
    k_specs, vt_specs, k_args, vt_args, keys = [], [], [], [], 0
    for k, vt in segs:
        if k.ndim == 4:
            n = k.shape[2]
            k_specs.append(pl.BlockSpec((None, MLA_HEADS, n, HEAD_PAD), lambda i, j: (i, 0, 0, 0)))
            vt_specs.append(pl.BlockSpec((None, MLA_HEADS, V_ROWS, n), lambda i, j: (i, 0, 0, 0)))
        else:
            n = k.shape[1] // batch
            k_specs.append(pl.BlockSpec((MLA_HEADS, n, HEAD_PAD), lambda i, j: (0, i, 0)))
            vt_specs.append(pl.BlockSpec((MLA_HEADS, V_ROWS, n), lambda i, j: (0, 0, i)))
        k_args.append(k)
        vt_args.append(vt)
        keys += n
    assert MLA_HEADS % 2 == 0
    return pl.pallas_call(
        functools.partial(_attn_kernel, n_seg=n_seg),
        grid=(batch, nq),
        in_specs=[
            pl.BlockSpec((MLA_HEADS, None, HEAD_PAD, tq), lambda i, j: (0, i * nq + j, 0, 0)),
            pl.BlockSpec((MLA_HEADS, None, HEAD_PAD, tq),
                         lambda i, j: (0, i * nq + jnp.minimum(j + 1, nq - 1), 0, 0)),
        ] + k_specs + vt_specs,
        out_specs=pl.BlockSpec((tq, MLA_W), lambda i, j: (i * nq + j, 0)),
        out_shape=jax.ShapeDtypeStruct((rows, MLA_W), BF16),
        scratch_shapes=[pltpu.VMEM((keys, tq), F32), pltpu.VMEM((keys, tq), F32),
                        pltpu.VMEM((1, tq), F32), pltpu.VMEM((1, tq), F32),
                        pltpu.VMEM((MLA_W, tq), F32)],
        compiler_params=_params(("arbitrary", "arbitrary")),
        name="mla_attention",
    )(qt, qt, *k_args, *vt_args)


def _attn_short_kernel(q_ref, k_ref, v_ref, o_ref):
    lane = lax.broadcasted_iota(jnp.int32, (1, HEAD_PAD), 1)
    for j in range(MLA_HEADS // 2):
        vs = v_ref[:, j * HEAD_PAD:(j + 1) * HEAD_PAD]
        outs = []
        for hd in (2 * j, 2 * j + 1):
            sl = slice(hd * HEAD_PAD, (hd + 1) * HEAD_PAD)
            s = _dot_nt(q_ref[:, sl], k_ref[:, sl])
            p = jnp.exp2(s - jnp.max(s, axis=-1, keepdims=True))
            l = jnp.sum(p, axis=-1, keepdims=True)
            outs.append(_dot(p.astype(BF16), vs) / l)
        o_ref[:, j * HEAD_PAD:(j + 1) * HEAD_PAD] = jnp.where(lane < V_HEAD, outs[0], outs[1]).astype(BF16)


def _attn_short_call(q, k, v, seq):
    rows = q.shape[0]
    return pl.pallas_call(
        _attn_short_kernel,
        grid=(rows // seq,),
        in_specs=[
            pl.BlockSpec((seq, QK_PAD), lambda i: (i, 0)),
            pl.BlockSpec((seq, QK_PAD), lambda i: (i, 0)),
            pl.BlockSpec((seq, MLA_W), lambda i: (i, 0)),
        ],
        out_specs=pl.BlockSpec((seq, MLA_W), lambda i: (i, 0)),
        out_shape=jax.ShapeDtypeStruct((rows, MLA_W), BF16),
        compiler_params=_params(("parallel",)),
        name="mla_attention_context",
    )(q, k, v)


def _merge_kernel(x_ref, mods_ref, gpre_ref, gpost_ref, z_ref, zprev_ref, znext_ref, att_ref, band_ref,
                  wpool_ref, pscale_ref, gsgu_ref, wsgu_ref, bsgu_ref, convw_ref, wgate_ref, bgate_ref,
                  wbp_ref, wbs_ref, wbm_ref, wbc_ref, wo_ref, o_ref, ext_ref, *, seqlen):
    tm = x_ref.shape[0]
    rows = _sub_tiles(tm)
    i = pl.program_id(0)
    lane = lax.broadcasted_iota(jnp.int32, (1, POOL_W), 1)
    grp = lane // POOL_GW
    half = jnp.where(grp == 0, POOL_HALF[0],
                     jnp.where(grp == 1, POOL_HALF[1],
                               jnp.where(grp == 2, POOL_HALF[2], POOL_HALF[3])))

    def mixer_in(ref, rs):
        return ref[rs, 0:256], ref[rs, 1024:1280] * ref[rs, 1280:1536]

    def fill(base, rs, before, after):
        n = rs.stop - rs.start
        for c, part in enumerate(zip(before, mixer_in(z_ref, rs), after)):
            cs = slice(c * 256, (c + 1) * 256)
            ext_ref[base:base + HALO, cs] = part[0]
            ext_ref[base + HALO:base + HALO + n, cs] = part[1]
            ext_ref[base + HALO + n:base + 2 * HALO + n, cs] = part[2]
        ext_ref[base + 2 * HALO + n:base + n + CHUNK, :] = jnp.zeros((CHUNK - 2 * HALO, 512), F32)

    zero_halo = (jnp.zeros((HALO, 256), F32),) * 2
    per_sequence = seqlen < tm
    if per_sequence:
        assert seqlen == SUB_ROWS
        bases = [r * (SUB_ROWS + CHUNK) for r in range(len(rows))]
    else:
        assert seqlen % tm == 0
        starts_seq = (i * tm) % seqlen == 0
        ends_seq = ((i + 1) * tm) % seqlen == 0
        every = slice(0, HALO)
        fill(0, slice(0, tm),
             tuple(jnp.where(starts_seq, 0.0, v) for v in mixer_in(zprev_ref, every)),
             tuple(jnp.where(ends_seq, 0.0, v) for v in mixer_in(znext_ref, every)))
        bases = [r * SUB_ROWS for r in range(len(rows))]

    def mix(r):
        rs, base = rows[r], bases[r]
        if per_sequence:
            fill(base, rs, zero_halo, zero_halo)
        zp, zu, zv, zb = (z_ref[rs, c * 256:(c + 1) * 256] for c in range(4))
        t = (lax.broadcasted_iota(jnp.int32, (SUB_ROWS, 1), 0) + (i * tm + rs.start)) & (seqlen - 1)
        sums = []
        for n in range(SUB_ROWS // CHUNK):
            lo = base + n * CHUNK
            xe = ext_ref[lo:lo + 2 * CHUNK, 0:256]
            hi = xe.astype(BF16)
            low = (xe - hi.astype(F32)).astype(BF16)
            sums.append(_by_group(grp, _dot(band_ref[...], hi) + _dot(band_ref[...], low)))
        cnt = (jnp.minimum(t + half, seqlen) - jnp.maximum(t - half, 0)).astype(F32)
        dpool = (jnp.concatenate(sums, axis=0) / cnt - zp).astype(BF16)
        a = (_dot(dpool, wpool_ref[...]) * pscale_ref[...]).astype(BF16)
        vc = (_rms(zv) * gsgu_ref[...]).astype(BF16)
        parts = []
        for n in range(SUB_ROWS // CHUNK):
            cs = slice(n * CHUNK, (n + 1) * CHUNK)
            mixed = _by_group(grp, _dot(wsgu_ref[...], vc[cs, :]))
            parts.append((zu[cs, :] * (mixed + bsgu_ref[...])).astype(BF16))
        b = jnp.concatenate(parts, axis=0)
        c0 = base + HALO
        conv = (ext_ref[c0 - 1:c0 - 1 + SUB_ROWS, 256:512] * convw_ref[0:1, :]
                + ext_ref[c0:c0 + SUB_ROWS, 256:512] * convw_ref[1:2, :]
                + ext_ref[c0 + 1:c0 + 1 + SUB_ROWS, 256:512] * convw_ref[2:3, :])
        return a, b, (zb * conv).astype(BF16)

    def pre(r):
        return _pre(x_ref[rows[r], :], gpre_ref[...], mods_ref, 1).astype(BF16)

    def post(r, m):
        o_ref[rows[r], :] = x_ref[rows[r], :] + mods_ref[5:6, :] * (_rms(m) * gpost_ref[...])

    h = pre(0)
    loc = mix(0)
    m_prev = None
    for r in range(len(rows)):
        branches = ((loc[0], wbp_ref), (loc[1], wbs_ref), (att_ref[rows[r], :], wbm_ref), (loc[2], wbc_ref))
        merged = None
        for j, (inp, w_ref) in enumerate(branches):
            sl = slice(j * D_MODEL, (j + 1) * D_MODEL)
            gate = _sigmoid(_dot(h, wgate_ref[:, sl]) + bgate_ref[:, sl])
            term = gate * _dot(inp, w_ref[...])
            merged = term if merged is None else merged + term
        if r + 1 < len(rows):
            h = pre(r + 1)
            loc = mix(r + 1)
        m = _dot(merged.astype(BF16), wo_ref[...])
        if r > 0:
            post(r - 1, m_prev)
        m_prev = m
    post(len(rows) - 1, m_prev)


def _merge_call(x, mods, cond, P, l, zloc, att, seqlen):
    rows = x.shape[0]
    tm = TM_MERGE
    n_sub = tm // SUB_ROWS
    ext_rows = n_sub * (SUB_ROWS + CHUNK) if seqlen < tm else tm + CHUNK
    hb = tm // HALO
    last = rows // HALO - 1
    row_spec = lambda w: pl.BlockSpec((tm, w), lambda i: (i, 0))
    local = ("wpool", "pool_scale", "g_sgu", "wsgu", "bsgu", "conv_w")
    names = ("w_gate", "b_gate", "w_br_pool", "w_br_sgu", "w_br_mla", "w_br_conv", "w_o")
    return pl.pallas_call(
        functools.partial(_merge_kernel, seqlen=seqlen),
        grid=(rows // tm,),
        in_specs=[
            row_spec(D_MODEL),
            _mods_spec(l, cond[0], rows // cond[1] // tm),
            _sel_spec(P["g_pre"], (l, 1)),
            _sel_spec(P["g_post"], (l, 1)),
            row_spec(LOC_W),
            pl.BlockSpec((HALO, LOC_W), lambda i: (jnp.maximum(i * hb - 1, 0), 0)),
            pl.BlockSpec((HALO, LOC_W), lambda i: (jnp.minimum((i + 1) * hb, last), 0)),
            row_spec(MLA_W),
            _sel_spec(P["band"]),
        ] + [_sel_spec(P[n], (l,)) for n in local + names],
        out_specs=row_spec(D_MODEL),
        out_shape=jax.ShapeDtypeStruct((rows, D_MODEL), F32),
        scratch_shapes=[pltpu.VMEM((ext_rows, 512), F32)],
        compiler_params=_params(("parallel",)),
        name="gated_merge",
    )(x, mods, P["g_pre"], P["g_post"], zloc, zloc, zloc, att, P["band"], *[P[n] for n in local + names])


def _rope_swap_perm():
    j = np.arange(QK_ROPE)
    return np.where((j % 16) < 8, j + 8, j - 8)


def _prepare(w_in, w_uq, w_ukv, w_pool, w_sgu, b_sgu):
    perm = _rope_swap_perm()

    zp, zu, zv, cq, ckv, kr, zb, zc, zx = jnp.split(
        w_in, np.cumsum((256, 256, 256, 256, 128, 32, 256, 256)), axis=2)
    pad_rope = lambda a: jnp.pad(a, ((0, 0),) * (a.ndim - 1) + (ROPE_PAD,))
    win = jnp.concatenate([zp, zu, zv, zb, zc, zx, cq, ckv, pad_rope(kr), pad_rope(kr[:, :, perm])],
                          axis=2).astype(BF16)

    uq = w_uq.reshape(DEPTH, Q_LORA, MLA_HEADS, QK_NOPE + QK_ROPE)
    wq = jnp.concatenate(
        [jnp.pad(uq, ((0, 0), (0, 0), (0, 0), (0, HEAD_PAD - QK_NOPE - QK_ROPE))).reshape(DEPTH, Q_LORA, QK_PAD),
         pad_rope(uq[:, :, :, QK_NOPE:][:, :, :, perm]).reshape(DEPTH, Q_LORA, QK_PAD)],
        axis=2).astype(BF16)
    wqt = wq.transpose(0, 2, 1)

    dkv = QK_NOPE + V_HEAD
    wk = jnp.where(jnp.asarray(np.arange(MLA_HEADS * dkv) % dkv < QK_NOPE), w_ukv, 0.0).astype(BF16)
    wv = w_ukv.reshape(DEPTH, KV_LORA, MLA_HEADS, dkv)[:, :, :, QK_NOPE:].reshape(DEPTH, KV_LORA, MLA_W).astype(BF16)
    wvt = wv.transpose(0, 2, 1)

    grp = np.arange(POOL_W) // POOL_GW
    wpool = jnp.where(jnp.asarray(grp[:, None] == grp[None, :]),
                      jnp.tile(w_pool.reshape(DEPTH, POOL_W, POOL_GW), (1, 1, POOL_GROUPS)), 0.0).astype(BF16)
    wsgu = w_sgu.reshape(DEPTH, SGU_GROUPS * CHUNK, CHUNK).astype(BF16)
    bsgu = jnp.repeat(b_sgu.transpose(0, 2, 1), SGU_W // SGU_GROUPS, axis=2)
    return dict(win=win, wq=wq, wqt=wqt, wk=wk, wv=wv, wvt=wvt,
                wpool=wpool, wsgu=wsgu, bsgu=bsgu, band=_pool_band())


def _rope_tables(n):
    f32 = np.float32
    rows = n // GRID_W
    r = np.repeat(np.arange(rows), GRID_W).astype(f32)
    col = np.tile(np.arange(GRID_W), rows).astype(f32)
    half = QK_ROPE // 2
    freqs = np.power(f32(ROPE_THETA), -(f32(2.0) * np.arange(half // 2, dtype=f32)) / f32(half)).astype(f32)
    ang = np.stack([r[:, None] * freqs, col[:, None] * freqs], axis=1).astype(f32)
    cos, sin = np.cos(ang).astype(f32), np.sin(ang).astype(f32)
    cos32 = np.concatenate([cos, cos], axis=-1).reshape(n, QK_ROPE)
    sin32 = np.concatenate([-sin, sin], axis=-1).reshape(n, QK_ROPE)
    cos_tab = np.concatenate([np.ones((n, QK_NOPE), f32), cos32,
                              np.zeros((n, HEAD_PAD - QK_NOPE - QK_ROPE), f32)], axis=1)
    sin_tab = np.pad(sin32, ((0, 0), ROPE_PAD))
    tabs = (cos_tab, sin_tab, np.ascontiguousarray(cos_tab.T), np.ascontiguousarray(sin_tab.T))
    return tuple(jnp.asarray(t) for t in tabs)


def kernel(x_prompt, x_sample, cache_ckv, cache_krope, c, c_ctx, w_mod, b_mod, g_pre, g_post,
           w_ffn_gu, w_ffn_dn, w_in, w_pool, pool_scale, g_sgu, w_sgu, b_sgu, g_q, w_uq, g_kv,
           w_ukv, conv_w, w_br_pool, w_br_sgu, w_br_mla, w_br_conv, w_gate, b_gate, w_o):
    batch, seq, _ = x_prompt.shape
    dec_batch, dec_seq, _ = x_sample.shape

    cond8 = jnp.concatenate([c_ctx[None, :], c, jnp.zeros((8 - 1 - dec_batch, D_MODEL), F32)], axis=0)
    mods = _mods_call(cond8, w_mod, b_mod).reshape(DEPTH, 8, N_MOD, D_MODEL)
    rope_tabs = _rope_tables(dec_seq)
    place = jnp.asarray(np.pad(np.eye(QK_ROPE, dtype=np.float32), ((0, 0), ROPE_PAD)), dtype=BF16)

    P = _prepare(w_in, w_uq, w_ukv, w_pool, w_sgu, b_sgu)
    row = lambda a: a.reshape(a.shape[:-1] + (1, a.shape[-1]))
    P.update(
        g_pre=row(g_pre), g_post=row(g_post), g_q=row(g_q), g_kv=row(g_kv), g_sgu=row(g_sgu),
        pool_scale=row(pool_scale), b_gate=row(b_gate), conv_w=conv_w,
        w_ffn_gu=w_ffn_gu.astype(BF16), w_ffn_dn=w_ffn_dn.astype(BF16), w_gate=w_gate.astype(BF16),
        w_br_pool=w_br_pool.astype(BF16), w_br_sgu=w_br_sgu.astype(BF16),
        w_br_mla=w_br_mla.astype(BF16), w_br_conv=w_br_conv.astype(BF16), w_o=w_o.astype(BF16),
    )

    streams = {"ctx": (x_prompt.reshape(batch * seq, D_MODEL), (0, 1), batch, seq),
               "lat": (x_sample.reshape(dec_batch * dec_seq, D_MODEL), (1, dec_batch), dec_batch, dec_seq)}
    outs = {}
    ckv_states, kr_states = [], []
    for name, (x, cond, nb, ns) in streams.items():
        is_ctx = name == "ctx"
        for l in range(DEPTH):
            x = _ffn_call(x, mods, cond, P, l, 0)
            front = _front_call(x, mods, cond, P, l, None if is_ctx else rope_tabs)
            zloc, q, k, v = front[:4]
            if is_ctx:
                ckv_states.append(front[4].reshape(batch, seq, KV_LORA))
                kr_states.append(front[5].reshape(batch, seq, QK_ROPE))
                att = _attn_short_call(q, k, v, ns)
            else:
                att = _attn_call(q, [_kvup_call(cache_ckv, cache_krope, P, l, place), (k, v)], nb)
            x = _merge_call(x, mods, cond, P, l, zloc, att, ns)
            x = _ffn_call(x, mods, cond, P, l, 2)
        outs[name] = x

    y_prompt = outs["ctx"].reshape(batch, seq, D_MODEL)
    y_sample = outs["lat"].reshape(dec_batch, dec_seq, D_MODEL)
    state_ckv = jnp.stack(ckv_states, axis=1)
    state_krope = jnp.stack(kr_states, axis=1)
    return (y_prompt, y_sample, state_ckv, state_krope)
```

```python
import functools

import numpy as np
import jax
import jax.numpy as jnp
from jax import lax
from jax.experimental import pallas as pl
from jax.experimental.pallas import tpu as pltpu

F32 = jnp.float32
BF16 = jnp.bfloat16

D_MODEL = 1024
DEPTH = 2
GRID_W = 64
N_MOD = 9
D_FF = 2816
EPS = 1e-6
POOL_GROUPS = 4
POOL_GW = 64
POOL_W = 256
SGU_GROUPS = 4
SGU_W = 256
CHUNK = 128
MLA_HEADS = 8
QK_NOPE = 64
QK_ROPE = 32
V_HEAD = 64
Q_LORA = 256
KV_LORA = 128
MLA_W = MLA_HEADS * V_HEAD
ROPE_THETA = 10000.0
CONV_W = 256
CONV_K = 3
N_BRANCH = 4
HEAD_PAD = 128
QK_PAD = MLA_HEADS * HEAD_PAD
ROPE_PAD = (QK_NOPE, HEAD_PAD - QK_NOPE - QK_ROPE)
LOC_W = 6 * 256
MAIN_W = LOC_W + Q_LORA + KV_LORA
HALO = 8
POOL_HALF = (1, 2, 4, 8)
LOG2E = 1.4426950408889634

VMEM_LIMIT = 56 * 1024 * 1024

TM_TOKEN = 1024
TM_MERGE = 512
SUB_ROWS = 256
TM_FRONT = 1024
V_ROWS = V_HEAD + 16
TQ_ATTN = 256
SEQS_PER_STEP = 4


def _sel_spec(arr, idx=(), cols=None):
    rest = arr.shape[len(idx):]
    if cols is not None:
        rest = rest[:-1] + (cols,)
    zeros = (0,) * len(rest)
    return pl.BlockSpec((None,) * len(idx) + rest, lambda *_: tuple(idx) + zeros,
                        pipeline_mode=pl.Buffered(1))


def _mods_spec(l, cond0, per_cond):
    return pl.BlockSpec((None, None, N_MOD, D_MODEL), lambda i: (l, cond0 + i // per_cond, 0, 0))


def _params(sem):
    return pltpu.CompilerParams(dimension_semantics=sem, vmem_limit_bytes=VMEM_LIMIT)


def _sigmoid(x):
    return 1.0 / (1.0 + jnp.exp(-x))


def _rms(x):
    return x * lax.rsqrt(jnp.mean(x * x, axis=-1, keepdims=True) + EPS)


def _pre(x, g, mods_ref, s):
    return (_rms(x) * g) * (1.0 + mods_ref[3 * s + 1:3 * s + 2, :]) + mods_ref[3 * s:3 * s + 1, :]


def _dot(a, b):
    return jnp.dot(a, b, preferred_element_type=F32)


def _dot_nt(a, b):
    return lax.dot_general(a, b, (((1,), (1,)), ((), ())), preferred_element_type=F32)


def _mods_kernel(cond_ref, w_ref, b_ref, o_ref):
    c = cond_ref[...]
    a = (c * _sigmoid(c)).astype(BF16)
    o_ref[0] = _dot(a, w_ref[0].astype(BF16)) + b_ref[0]


def _mods_call(cond8, w_mod, b_mod):
    n_tile = 3 * D_MODEL
    nt = (N_MOD * D_MODEL) // n_tile
    return pl.pallas_call(
        _mods_kernel,
        grid=(DEPTH, nt),
        in_specs=[
            pl.BlockSpec((8, D_MODEL), lambda l, j: (0, 0)),
            pl.BlockSpec((1, D_MODEL, n_tile), lambda l, j: (l, 0, j)),
            pl.BlockSpec((1, 1, n_tile), lambda l, j: (l, 0, j)),
        ],
        out_specs=pl.BlockSpec((1, 8, n_tile), lambda l, j: (l, 0, j)),
        out_shape=jax.ShapeDtypeStruct((DEPTH, 8, N_MOD * D_MODEL), F32),
        compiler_params=_params(("parallel", "parallel")),
        name="adaln_mods",
    )(cond8, w_mod, b_mod.reshape(DEPTH, 1, N_MOD * D_MODEL))


def _sub_tiles(tm, sub_rows=SUB_ROWS):
    return [slice(lo, lo + sub_rows) for lo in range(0, tm, sub_rows)]


def _ffn_kernel(x_ref, mods_ref, gpre_ref, gpost_ref, wgu_ref, wdn_ref, o_ref, *, s):
    rows = _sub_tiles(x_ref.shape[0])
    gate = mods_ref[3 * s + 2:3 * s + 3, :]

    def pre(r):
        return _pre(x_ref[rows[r], :], gpre_ref[...], mods_ref, s).astype(BF16)

    def post(r, f):
        o_ref[rows[r], :] = x_ref[rows[r], :] + (0.5 * gate) * (_rms(f) * gpost_ref[...])

    h = pre(0)
    f_prev = None
    for r in range(len(rows)):
        g = _dot(h, wgu_ref[:, :D_FF])
        u = _dot(h, wgu_ref[:, D_FF:])
        a = ((g * _sigmoid(g)) * u).astype(BF16)
        if r + 1 < len(rows):
            h = pre(r + 1)
        f = _dot(a, wdn_ref[...])
        if r > 0:
            post(r - 1, f_prev)
        f_prev = f
    post(len(rows) - 1, f_prev)


def _ffn_call(x, mods, cond, P, l, s):
    rows = x.shape[0]
    tm = TM_TOKEN
    j = s // 2
    return pl.pallas_call(
        functools.partial(_ffn_kernel, s=s),
        grid=(rows // tm,),
        in_specs=[
            pl.BlockSpec((tm, D_MODEL), lambda i: (i, 0)),
            _mods_spec(l, cond[0], rows // cond[1] // tm),
            _sel_spec(P["g_pre"], (l, s)),
            _sel_spec(P["g_post"], (l, s)),
            _sel_spec(P["w_ffn_gu"], (l, j)),
            _sel_spec(P["w_ffn_dn"], (l, j)),
        ],
        out_specs=pl.BlockSpec((tm, D_MODEL), lambda i: (i, 0)),
        out_shape=jax.ShapeDtypeStruct((rows, D_MODEL), F32),
        compiler_params=_params(("parallel",)),
        name="ffn_half_step",
    )(x, mods, P["g_pre"], P["g_post"], P["w_ffn_gu"], P["w_ffn_dn"])


def _front_kernel(*refs, latent):
    if latent:
        (x_ref, mods_ref, gpre_ref, win_ref, gq_ref, wq_ref, gkv_ref, wk_ref, wv_ref,
         cos_ref, sin_ref, cost_ref, sint_ref, zloc_ref, q_ref, k_ref, v_ref) = refs
    else:
        (x_ref, mods_ref, gpre_ref, win_ref, gq_ref, wq_ref, gkv_ref, wk_ref, wv_ref,
         zloc_ref, q_ref, k_ref, v_ref, ckv_ref, kr_ref) = refs
    q_scale = float(QK_NOPE + QK_ROPE) ** -0.5 * LOG2E
    tm = x_ref.shape[0]

    def project(rs, z):
        zloc_ref[rs, :] = z[:, :LOC_W]
        qn = (_rms(z[:, LOC_W:LOC_W + Q_LORA]) * gq_ref[...]).astype(BF16)
        ckv_n = _rms(z[:, LOC_W + Q_LORA:MAIN_W]) * gkv_ref[...]
        kr = z[:, MAIN_W:MAIN_W + HEAD_PAD]
        ckv_b = ckv_n.astype(BF16)
        kn = _dot(ckv_b, wk_ref[...])
        if latent:
            kr = kr * cos_ref[rs, :] + z[:, MAIN_W + HEAD_PAD:] * sin_ref[rs, :]
            vt = _dot_nt(wv_ref[...], ckv_b)
            qt = _dot_nt(wq_ref[0:QK_PAD, :], qn)
            qt_sw = _dot_nt(wq_ref[QK_PAD:, :], qn)
            cost = cost_ref[:, rs]
            sint = sint_ref[:, rs]
            ones = jnp.ones((V_ROWS - V_HEAD, tm), BF16)
            for hd in range(MLA_HEADS):
                sl = slice(hd * HEAD_PAD, (hd + 1) * HEAD_PAD)
                qh = ((qt[sl, :] * cost + qt_sw[sl, :] * sint) * q_scale).astype(BF16)
                for c in range(tm // TQ_ATTN):
                    q_ref[hd, c] = qh[:, c * TQ_ATTN:(c + 1) * TQ_ATTN]
                k_ref[hd, rs, :] = (kn[:, sl] + kr).astype(BF16)
                v_ref[hd, 0:V_HEAD, rs] = vt[hd * V_HEAD:(hd + 1) * V_HEAD, :].astype(BF16)
                v_ref[hd, V_HEAD:, rs] = ones
        else:
            ckv_ref[rs, :] = ckv_n
            kr_ref[rs, :] = kr[:, QK_NOPE:QK_NOPE + QK_ROPE]
            v_ref[rs, :] = _dot(ckv_b, wv_ref[...]).astype(BF16)
            q_ref[rs, :] = (_dot(qn, wq_ref[...]) * q_scale).astype(BF16)
            for hd in range(MLA_HEADS):
                sl = slice(hd * HEAD_PAD, (hd + 1) * HEAD_PAD)
                k_ref[rs, sl] = (kn[:, sl] + kr).astype(BF16)

    h = _pre(x_ref[...], gpre_ref[...], mods_ref, 1).astype(BF16)
    project(slice(0, tm), _dot(h, win_ref[...]))


def _front_call(x, mods, cond, P, l, rope_tabs):
    rows = x.shape[0]
    tm = TM_FRONT
    latent = rope_tabs is not None
    wq = P["wqt"] if latent else P["wq"]
    wv = P["wvt"] if latent else P["wv"]
    row_spec = lambda w: pl.BlockSpec((tm, w), lambda i: (i, 0))
    in_specs = [
        row_spec(D_MODEL),
        _mods_spec(l, cond[0], rows // cond[1] // tm),
        _sel_spec(P["g_pre"], (l, 1)),
        _sel_spec(P["win"], (l,), None if latent else MAIN_W + HEAD_PAD),
        _sel_spec(P["g_q"], (l,)),
        _sel_spec(wq, (l,), None if latent else QK_PAD),
        _sel_spec(P["g_kv"], (l,)),
        _sel_spec(P["wk"], (l,)),
        _sel_spec(wv, (l,)),
    ]
    args = [x, mods, P["g_pre"], P["win"], P["g_q"], wq, P["g_kv"], P["wk"], wv]
    if latent:
        cos_tab, sin_tab, cos_tab_t, sin_tab_t = rope_tabs
        seq_tiles = cos_tab.shape[0] // tm
        in_specs += [pl.BlockSpec((tm, HEAD_PAD), lambda i: (i % seq_tiles, 0))] * 2
        in_specs += [pl.BlockSpec((HEAD_PAD, tm), lambda i: (0, i % seq_tiles))] * 2
        args += [cos_tab, sin_tab, cos_tab_t, sin_tab_t]
        out_specs = [
            row_spec(LOC_W),
            pl.BlockSpec((MLA_HEADS, tm // TQ_ATTN, HEAD_PAD, TQ_ATTN), lambda i: (0, i, 0, 0)),
            pl.BlockSpec((MLA_HEADS, tm, HEAD_PAD), lambda i: (0, i, 0)),
            pl.BlockSpec((MLA_HEADS, V_ROWS, tm), lambda i: (0, 0, i)),
        ]
        out_shape = [
            jax.ShapeDtypeStruct((rows, LOC_W), F32),
            jax.ShapeDtypeStruct((MLA_HEADS, rows // TQ_ATTN, HEAD_PAD, TQ_ATTN), BF16),
            jax.ShapeDtypeStruct((MLA_HEADS, rows, HEAD_PAD), BF16),
            jax.ShapeDtypeStruct((MLA_HEADS, V_ROWS, rows), BF16),
        ]
    else:
        out_specs = [row_spec(LOC_W), row_spec(QK_PAD), row_spec(QK_PAD), row_spec(MLA_W),
                     row_spec(KV_LORA), row_spec(QK_ROPE)]
        out_shape = [
            jax.ShapeDtypeStruct((rows, LOC_W), F32),
            jax.ShapeDtypeStruct((rows, QK_PAD), BF16),
            jax.ShapeDtypeStruct((rows, QK_PAD), BF16),
            jax.ShapeDtypeStruct((rows, MLA_W), BF16),
            jax.ShapeDtypeStruct((rows, KV_LORA), F32),
            jax.ShapeDtypeStruct((rows, QK_ROPE), F32),
        ]
    return pl.pallas_call(
        functools.partial(_front_kernel, latent=latent),
        grid=(rows // tm,),
        in_specs=in_specs,
        out_specs=out_specs,
        out_shape=out_shape,
        compiler_params=_params(("parallel",)),
        name="mix_front_latent" if latent else "mix_front_context",
    )(*args)


def _kvup_kernel(ckv_ref, kr_ref, wk_ref, wvt_ref, place_ref, k_ref, vt_ref):
    ckv_b = ckv_ref[...].astype(BF16)
    kn = _dot(ckv_b, wk_ref[...])
    kr = _dot(kr_ref[...].astype(BF16), place_ref[...])
    vt = _dot_nt(wvt_ref[...], ckv_b)
    ones = jnp.ones((V_ROWS - V_HEAD, ckv_b.shape[0]), BF16)
    for hd in range(MLA_HEADS):
        sl = slice(hd * HEAD_PAD, (hd + 1) * HEAD_PAD)
        k_ref[hd] = (kn[:, sl] + kr).astype(BF16)
        vt_ref[hd, 0:V_HEAD, :] = vt[hd * V_HEAD:(hd + 1) * V_HEAD, :].astype(BF16)
        vt_ref[hd, V_HEAD:, :] = ones


def _kvup_call(cache_ckv, cache_krope, P, l, place):
    b, _, n, _ = cache_ckv.shape
    return pl.pallas_call(
        _kvup_kernel,
        grid=(b,),
        in_specs=[
            pl.BlockSpec((None, None, n, KV_LORA), lambda i: (i, l, 0, 0)),
            pl.BlockSpec((None, None, n, QK_ROPE), lambda i: (i, l, 0, 0)),
            _sel_spec(P["wk"], (l,)),
            _sel_spec(P["wvt"], (l,)),
            _sel_spec(place),
        ],
        out_specs=[pl.BlockSpec((None, MLA_HEADS, n, HEAD_PAD), lambda i: (i, 0, 0, 0)),
                   pl.BlockSpec((None, MLA_HEADS, V_ROWS, n), lambda i: (i, 0, 0, 0))],
        out_shape=[jax.ShapeDtypeStruct((b, MLA_HEADS, n, HEAD_PAD), BF16),
                   jax.ShapeDtypeStruct((b, MLA_HEADS, V_ROWS, n), BF16)],
        compiler_params=_params(("parallel",)),
        name="cache_kv_up",
    )(cache_ckv, cache_krope, P["wk"], P["wvt"], place)


def _by_group(grp, r):
    return jnp.where(grp == 0, r[0:CHUNK],
                     jnp.where(grp == 1, r[CHUNK:2 * CHUNK],
                               jnp.where(grp == 2, r[2 * CHUNK:3 * CHUNK], r[3 * CHUNK:])))


def _pool_band():
    p = np.arange(CHUNK)[:, None]
    j = np.arange(2 * CHUNK)[None, :]
    bands = [((j >= p + HALO - hw) & (j < p + HALO + hw)) for hw in POOL_HALF]
    return jnp.asarray(np.concatenate(bands, axis=0).astype(np.float32), dtype=BF16)


def _attn_kernel(*refs, n_seg):
    qt_ref, qtn_ref = refs[0:2]
    k_refs = refs[2:2 + n_seg]
    vt_refs = refs[2 + n_seg:2 + 2 * n_seg]
    o_ref, sa_ref, sb_ref, ma_ref, mb_ref, ot_ref = refs[2 + 2 * n_seg:]
    buf_a = (sa_ref, ma_ref)
    buf_b = (sb_ref, mb_ref)
    bounds = [0]
    for k_ref in k_refs:
        bounds.append(bounds[-1] + k_ref.shape[1])

    def scores(q_ref, h, buf):
        s_ref, m_ref = buf
        qt = q_ref[h]
        m = None
        for k_ref, lo, hi in zip(k_refs, bounds[:-1], bounds[1:]):
            s = _dot(k_ref[h], qt)
            s_ref[lo:hi, :] = s
            ms = jnp.max(s, axis=0, keepdims=True)
            m = ms if m is None else jnp.maximum(m, ms)
        m_ref[...] = m

    def values(h, buf):
        s_ref, m_ref = buf
        m = m_ref[...]
        ot = None
        for vt_ref, lo, hi in zip(vt_refs, bounds[:-1], bounds[1:]):
            p = jnp.exp2(s_ref[lo:hi, :] - m)
            os_ = _dot(vt_ref[h], p.astype(BF16))
            ot = os_ if ot is None else ot + os_
        ot_ref[pl.ds(pl.multiple_of(h * V_HEAD, V_HEAD), V_HEAD), :] = ot[0:V_HEAD] / ot[V_HEAD:V_HEAD + 1]

    @pl.when(pl.program_id(1) == 0)
    def _():
        scores(qt_ref, 0, buf_a)

    def body(h, carry):
        @pl.when(h % 2 == 0)
        def _():
            scores(qt_ref, h + 1, buf_b)
            values(h, buf_a)

        @pl.when(h % 2 == 1)
        def _():
            scores(qt_ref, h + 1, buf_a)
            values(h, buf_b)
        return carry

    lax.fori_loop(0, MLA_HEADS - 1, body, 0)
    scores(qtn_ref, 0, buf_a)
    values(MLA_HEADS - 1, buf_b)
    o_ref[...] = ot_ref[...].T.astype(BF16)


def _attn_call(qt, segs, batch):
    tq = TQ_ATTN
    rows = qt.shape[1] * tq
    lq = rows // batch
    nq = lq // tq
    n_seg = len(segs)
    k_specs, vt_specs, k_args, vt_args, keys = [], [], [], [], 0
    for k, vt in segs:
        if k.ndim == 4:
            n = k.shape[2]
            k_specs.append(pl.BlockSpec((None, MLA_HEADS, n, HEAD_PAD), lambda i, j: (i, 0, 0, 0)))
            vt_specs.append(pl.BlockSpec((None, MLA_HEADS, V_ROWS, n), lambda i, j: (i, 0, 0, 0)))
        else:
            n = k.shape[1] // batch
            k_specs.append(pl.BlockSpec((MLA_HEADS, n, HEAD_PAD), lambda i, j: (0, i, 0)))
            vt_specs.append(pl.BlockSpec((MLA_HEADS, V_ROWS, n), lambda i, j: (0, 0, i)))
        k_args.append(k)
        vt_args.append(vt)
        keys += n
    assert MLA_HEADS % 2 == 0
    return pl.pallas_call(
        functools.partial(_attn_kernel, n_seg=n_seg),
        grid=(batch, nq),
        in_specs=[
            pl.BlockSpec((MLA_HEADS, None, HEAD_PAD, tq), lambda i, j: (0, i * nq + j, 0, 0)),
            pl.BlockSpec((MLA_HEADS, None, HEAD_PAD, tq),
                         lambda i, j: (0, i * nq + jnp.minimum(j + 1, nq - 1), 0, 0)),
        ] + k_specs + vt_specs,
        out_specs=pl.BlockSpec((tq, MLA_W), lambda i, j: (i * nq + j, 0)),
        out_shape=jax.ShapeDtypeStruct((rows, MLA_W), BF16),
        scratch_shapes=[pltpu.VMEM((keys, tq), F32), pltpu.VMEM((keys, tq), F32),
                        pltpu.VMEM((1, tq), F32), pltpu.VMEM((1, tq), F32),
                        pltpu.VMEM((MLA_W, tq), F32)],
        compiler_params=_params(("arbitrary", "arbitrary")),
        name="mla_attention",
    )(qt, qt, *k_args, *vt_args)


def _attn_short_kernel(q_ref, k_ref, v_ref, o_ref, *, seq):
    lane = lax.broadcasted_iota(jnp.int32, (1, HEAD_PAD), 1)
    for b in range(q_ref.shape[0] // seq):
        rs = slice(b * seq, (b + 1) * seq)
        for j in range(MLA_HEADS // 2):
            vs = v_ref[rs, j * HEAD_PAD:(j + 1) * HEAD_PAD]
            outs = []
            for hd in (2 * j, 2 * j + 1):
                sl = slice(hd * HEAD_PAD, (hd + 1) * HEAD_PAD)
                s = _dot_nt(q_ref[rs, sl], k_ref[rs, sl])
                p = jnp.exp2(s - jnp.max(s, axis=-1, keepdims=True))
                l = jnp.sum(p, axis=-1, keepdims=True)
                outs.append(_dot(p.astype(BF16), vs) / l)
            o_ref[rs, j * HEAD_PAD:(j + 1) * HEAD_PAD] = jnp.where(lane < V_HEAD, outs[0], outs[1]).astype(BF16)


def _attn_short_call(q, k, v, seq):
    rows = q.shape[0]
    tm = SEQS_PER_STEP * seq
    return pl.pallas_call(
        functools.partial(_attn_short_kernel, seq=seq),
        grid=(rows // tm,),
        in_specs=[
            pl.BlockSpec((tm, QK_PAD), lambda i: (i, 0)),
            pl.BlockSpec((tm, QK_PAD), lambda i: (i, 0)),
            pl.BlockSpec((tm, MLA_W), lambda i: (i, 0)),
        ],
        out_specs=pl.BlockSpec((tm, MLA_W), lambda i: (i, 0)),
        out_shape=jax.ShapeDtypeStruct((rows, MLA_W), BF16),
        compiler_params=_params(("parallel",)),
        name="mla_attention_context",
    )(q, k, v)


def _merge_kernel(x_ref, mods_ref, gpre_ref, gpost_ref, z_ref, zprev_ref, znext_ref, att_ref, band_ref,
                  wpool_ref, pscale_ref, gsgu_ref, wsgu_ref, bsgu_ref, convw_ref, wgate_ref, bgate_ref,
                  wbp_ref, wbs_ref, wbm_ref, wbc_ref, wo_ref, o_ref, ext_ref, *, seqlen):
    tm = x_ref.shape[0]
    rows = _sub_tiles(tm)
    i = pl.program_id(0)
    lane = lax.broadcasted_iota(jnp.int32, (1, POOL_W), 1)
    grp = lane // POOL_GW
    half = jnp.where(grp == 0, POOL_HALF[0],
                     jnp.where(grp == 1, POOL_HALF[1],
                               jnp.where(grp == 2, POOL_HALF[2], POOL_HALF[3])))

    def mixer_in(ref, rs):
        return ref[rs, 0:256], ref[rs, 1024:1280] * ref[rs, 1280:1536]

    def fill(base, rs, before, after):
        n = rs.stop - rs.start
        for c, part in enumerate(zip(before, mixer_in(z_ref, rs), after)):
            cs = slice(c * 256, (c + 1) * 256)
            ext_ref[base:base + HALO, cs] = part[0]
            ext_ref[base + HALO:base + HALO + n, cs] = part[1]
            ext_ref[base + HALO + n:base + 2 * HALO + n, cs] = part[2]
        ext_ref[base + 2 * HALO + n:base + n + CHUNK, :] = jnp.zeros((CHUNK - 2 * HALO, 512), F32)

    zero_halo = (jnp.zeros((HALO, 256), F32),) * 2
    per_sequence = seqlen < tm
    if per_sequence:
        assert seqlen == SUB_ROWS
        bases = [r * (SUB_ROWS + CHUNK) for r in range(len(rows))]
    else:
        assert seqlen % tm == 0
        starts_seq = (i * tm) % seqlen == 0
        ends_seq = ((i + 1) * tm) % seqlen == 0
        every = slice(0, HALO)
        fill(0, slice(0, tm),
             tuple(jnp.where(starts_seq, 0.0, v) for v in mixer_in(zprev_ref, every)),
             tuple(jnp.where(ends_seq, 0.0, v) for v in mixer_in(znext_ref, every)))
        bases = [r * SUB_ROWS for r in range(len(rows))]

    def mix(r):
        rs, base = rows[r], bases[r]
        if per_sequence:
            fill(base, rs, zero_halo, zero_halo)
        zp, zu, zv, zb = (z_ref[rs, c * 256:(c + 1) * 256] for c in range(4))
        t = (lax.broadcasted_iota(jnp.int32, (SUB_ROWS, 1), 0) + (i * tm + rs.start)) & (seqlen - 1)
        sums = []
        for n in range(SUB_ROWS // CHUNK):
            lo = base + n * CHUNK
            xe = ext_ref[lo:lo + 2 * CHUNK, 0:256]
            hi = xe.astype(BF16)
            low = (xe - hi.astype(F32)).astype(BF16)
            sums.append(_by_group(grp, _dot(band_ref[...], hi) + _dot(band_ref[...], low)))
        cnt = (jnp.minimum(t + half, seqlen) - jnp.maximum(t - half, 0)).astype(F32)
        dpool = (jnp.concatenate(sums, axis=0) / cnt - zp).astype(BF16)
        a = (_dot(dpool, wpool_ref[...]) * pscale_ref[...]).astype(BF16)
        vc = (_rms(zv) * gsgu_ref[...]).astype(BF16)
        parts = []
        for n in range(SUB_ROWS // CHUNK):
            cs = slice(n * CHUNK, (n + 1) * CHUNK)
            mixed = _by_group(grp, _dot(wsgu_ref[...], vc[cs, :]))
            parts.append((zu[cs, :] * (mixed + bsgu_ref[...])).astype(BF16))
        b = jnp.concatenate(parts, axis=0)
        c0 = base + HALO
        conv = (ext_ref[c0 - 1:c0 - 1 + SUB_ROWS, 256:512] * convw_ref[0:1, :]
                + ext_ref[c0:c0 + SUB_ROWS, 256:512] * convw_ref[1:2, :]
                + ext_ref[c0 + 1:c0 + 1 + SUB_ROWS, 256:512] * convw_ref[2:3, :])
        return a, b, (zb * conv).astype(BF16)

    def pre(r):
        return _pre(x_ref[rows[r], :], gpre_ref[...], mods_ref, 1).astype(BF16)

    def post(r, m):
        o_ref[rows[r], :] = x_ref[rows[r], :] + mods_ref[5:6, :] * (_rms(m) * gpost_ref[...])

    h = pre(0)
    loc = mix(0)
    m_prev = None
    for r in range(len(rows)):
        branches = ((loc[0], wbp_ref), (loc[1], wbs_ref), (att_ref[rows[r], :], wbm_ref), (loc[2], wbc_ref))
        merged = None
        for j, (inp, w_ref) in enumerate(branches):
            sl = slice(j * D_MODEL, (j + 1) * D_MODEL)
            gate = _sigmoid(_dot(h, wgate_ref[:, sl]) + bgate_ref[:, sl])
            term = gate * _dot(inp, w_ref[...])
            merged = term if merged is None else merged + term
        if r + 1 < len(rows):
            h = pre(r + 1)
            loc = mix(r + 1)
        m = _dot(merged.astype(BF16), wo_ref[...])
        if r > 0:
            post(r - 1, m_prev)
        m_prev = m
    post(len(rows) - 1, m_prev)


def _merge_call(x, mods, cond, P, l, zloc, att, seqlen):
    rows = x.shape[0]
    tm = TM_MERGE
    n_sub = tm // SUB_ROWS
    ext_rows = n_sub * (SUB_ROWS + CHUNK) if seqlen < tm else tm + CHUNK
    hb = tm // HALO
    last = rows // HALO - 1
    row_spec = lambda w: pl.BlockSpec((tm, w), lambda i: (i, 0))
    local = ("wpool", "pool_scale", "g_sgu", "wsgu", "bsgu", "conv_w")
    names = ("w_gate", "b_gate", "w_br_pool", "w_br_sgu", "w_br_mla", "w_br_conv", "w_o")
    return pl.pallas_call(
        functools.partial(_merge_kernel, seqlen=seqlen),
        grid=(rows // tm,),
        in_specs=[
            row_spec(D_MODEL),
            _mods_spec(l, cond[0], rows // cond[1] // tm),
            _sel_spec(P["g_pre"], (l, 1)),
            _sel_spec(P["g_post"], (l, 1)),
            row_spec(LOC_W),
            pl.BlockSpec((HALO, LOC_W), lambda i: (jnp.maximum(i * hb - 1, 0), 0)),
            pl.BlockSpec((HALO, LOC_W), lambda i: (jnp.minimum((i + 1) * hb, last), 0)),
            row_spec(MLA_W),
            _sel_spec(P["band"]),
        ] + [_sel_spec(P[n], (l,)) for n in local + names],
        out_specs=row_spec(D_MODEL),
        out_shape=jax.ShapeDtypeStruct((rows, D_MODEL), F32),
        scratch_shapes=[pltpu.VMEM((ext_rows, 512), F32)],
        compiler_params=_params(("parallel",)),
        name="gated_merge",
    )(x, mods, P["g_pre"], P["g_post"], zloc, zloc, zloc, att, P["band"], *[P[n] for n in local + names])


def _rope_swap_perm():
    j = np.arange(QK_ROPE)
    return np.where((j % 16) < 8, j + 8, j - 8)


def _prepare(w_in, w_uq, w_ukv, w_pool, w_sgu, b_sgu):
    perm = _rope_swap_perm()

    zp, zu, zv, cq, ckv, kr, zb, zc, zx = jnp.split(
        w_in, np.cumsum((256, 256, 256, 256, 128, 32, 256, 256)), axis=2)
    pad_rope = lambda a: jnp.pad(a, ((0, 0),) * (a.ndim - 1) + (ROPE_PAD,))
    win = jnp.concatenate([zp, zu, zv, zb, zc, zx, cq, ckv, pad_rope(kr), pad_rope(kr[:, :, perm])],
                          axis=2).astype(BF16)

    uq = w_uq.reshape(DEPTH, Q_LORA, MLA_HEADS, QK_NOPE + QK_ROPE)
    wq = jnp.concatenate(
        [jnp.pad(uq, ((0, 0), (0, 0), (0, 0), (0, HEAD_PAD - QK_NOPE - QK_ROPE))).reshape(DEPTH, Q_LORA, QK_PAD),
         pad_rope(uq[:, :, :, QK_NOPE:][:, :, :, perm]).reshape(DEPTH, Q_LORA, QK_PAD)],
        axis=2).astype(BF16)
    wqt = wq.transpose(0, 2, 1)

    dkv = QK_NOPE + V_HEAD
    wk = jnp.where(jnp.asarray(np.arange(MLA_HEADS * dkv) % dkv < QK_NOPE), w_ukv, 0.0).astype(BF16)
    wv = w_ukv.reshape(DEPTH, KV_LORA, MLA_HEADS, dkv)[:, :, :, QK_NOPE:].reshape(DEPTH, KV_LORA, MLA_W).astype(BF16)
    wvt = wv.transpose(0, 2, 1)

    grp = np.arange(POOL_W) // POOL_GW
    wpool = jnp.where(jnp.asarray(grp[:, None] == grp[None, :]),
                      jnp.tile(w_pool.reshape(DEPTH, POOL_W, POOL_GW), (1, 1, POOL_GROUPS)), 0.0).astype(BF16)
    wsgu = w_sgu.reshape(DEPTH, SGU_GROUPS * CHUNK, CHUNK).astype(BF16)
    bsgu = jnp.repeat(b_sgu.transpose(0, 2, 1), SGU_W // SGU_GROUPS, axis=2)
    return dict(win=win, wq=wq, wqt=wqt, wk=wk, wv=wv, wvt=wvt,
                wpool=wpool, wsgu=wsgu, bsgu=bsgu, band=_pool_band())


def _rope_tables(n):
    f32 = np.float32
    rows = n // GRID_W
    r = np.repeat(np.arange(rows), GRID_W).astype(f32)
    col = np.tile(np.arange(GRID_W), rows).astype(f32)
    half = QK_ROPE // 2
    freqs = np.power(f32(ROPE_THETA), -(f32(2.0) * np.arange(half // 2, dtype=f32)) / f32(half)).astype(f32)
    ang = np.stack([r[:, None] * freqs, col[:, None] * freqs], axis=1).astype(f32)
    cos, sin = np.cos(ang).astype(f32), np.sin(ang).astype(f32)
    cos32 = np.concatenate([cos, cos], axis=-1).reshape(n, QK_ROPE)
    sin32 = np.concatenate([-sin, sin], axis=-1).reshape(n, QK_ROPE)
    cos_tab = np.concatenate([np.ones((n, QK_NOPE), f32), cos32,
                              np.zeros((n, HEAD_PAD - QK_NOPE - QK_ROPE), f32)], axis=1)
    sin_tab = np.pad(sin32, ((0, 0), ROPE_PAD))
    tabs = (cos_tab, sin_tab, np.ascontiguousarray(cos_tab.T), np.ascontiguousarray(sin_tab.T))
    return tuple(jnp.asarray(t) for t in tabs)


def kernel(x_prompt, x_sample, cache_ckv, cache_krope, c, c_ctx, w_mod, b_mod, g_pre, g_post,
           w_ffn_gu, w_ffn_dn, w_in, w_pool, pool_scale, g_sgu, w_sgu, b_sgu, g_q, w_uq, g_kv,
           w_ukv, conv_w, w_br_pool, w_br_sgu, w_br_mla, w_br_conv, w_gate, b_gate, w_o):
    batch, seq, _ = x_prompt.shape
    dec_batch, dec_seq, _ = x_sample.shape

    cond8 = jnp.concatenate([c_ctx[None, :], c, jnp.zeros((8 - 1 - dec_batch, D_MODEL), F32)], axis=0)
    mods = _mods_call(cond8, w_mod, b_mod).reshape(DEPTH, 8, N_MOD, D_MODEL)
    rope_tabs = _rope_tables(dec_seq)
    place = jnp.asarray(np.pad(np.eye(QK_ROPE, dtype=np.float32), ((0, 0), ROPE_PAD)), dtype=BF16)

    P = _prepare(w_in, w_uq, w_ukv, w_pool, w_sgu, b_sgu)
    row = lambda a: a.reshape(a.shape[:-1] + (1, a.shape[-1]))
    P.update(
        g_pre=row(g_pre), g_post=row(g_post), g_q=row(g_q), g_kv=row(g_kv), g_sgu=row(g_sgu),
        pool_scale=row(pool_scale), b_gate=row(b_gate), conv_w=conv_w,
        w_ffn_gu=w_ffn_gu.astype(BF16), w_ffn_dn=w_ffn_dn.astype(BF16), w_gate=w_gate.astype(BF16),
        w_br_pool=w_br_pool.astype(BF16), w_br_sgu=w_br_sgu.astype(BF16),
        w_br_mla=w_br_mla.astype(BF16), w_br_conv=w_br_conv.astype(BF16), w_o=w_o.astype(BF16),
    )

    streams = {"ctx": (x_prompt.reshape(batch * seq, D_MODEL), (0, 1), batch, seq),
               "lat": (x_sample.reshape(dec_batch * dec_seq, D_MODEL), (1, dec_batch), dec_batch, dec_seq)}
    outs = {}
    ckv_states, kr_states = [], []
    for name, (x, cond, nb, ns) in streams.items():
        is_ctx = name == "ctx"
        for l in range(DEPTH):
            x = _ffn_call(x, mods, cond, P, l, 0)
            front = _front_call(x, mods, cond, P, l, None if is_ctx else rope_tabs)
            zloc, q, k, v = front[:4]
            if is_ctx:
                ckv_states.append(front[4].reshape(batch, seq, KV_LORA))
                kr_states.append(front[5].reshape(batch, seq, QK_ROPE))
                att = _attn_short_call(q, k, v, ns)
            else:
                att = _attn_call(q, [_kvup_call(cache_ckv, cache_krope, P, l, place), (k, v)], nb)
            x = _merge_call(x, mods, cond, P, l, zloc, att, ns)
            x = _ffn_call(x, mods, cond, P, l, 2)
        outs[name] = x

    y_prompt = outs["ctx"].reshape(batch, seq, D_MODEL)
    y_sample = outs["lat"].reshape(dec_batch, dec_seq, D_MODEL)
    state_ckv = jnp.stack(ckv_states, axis=1)
    state_krope = jnp.stack(kr_states, axis=1)
    return (y_prompt, y_sample, state_ckv, state_krope)
```

```python
import functools

import numpy as np
import jax
import jax.numpy as jnp
from jax import lax
from jax.experimental import pallas as pl
from jax.experimental.pallas import tpu as pltpu

F32 = jnp.float32
BF16 = jnp.bfloat16

D_MODEL = 1024
DEPTH = 2
GRID_W = 64
N_MOD = 9
D_FF = 2816
EPS = 1e-6
POOL_GROUPS = 4
POOL_GW = 64
POOL_W = 256
SGU_GROUPS = 4
SGU_W = 256
CHUNK = 128
MLA_HEADS = 8
QK_NOPE = 64
QK_ROPE = 32
V_HEAD = 64
Q_LORA = 256
KV_LORA = 128
MLA_W = MLA_HEADS * V_HEAD
ROPE_THETA = 10000.0
CONV_W = 256
CONV_K = 3
N_BRANCH = 4
HEAD_PAD = 128
QK_PAD = MLA_HEADS * HEAD_PAD
ROPE_PAD = (QK_NOPE, HEAD_PAD - QK_NOPE - QK_ROPE)
LOC_W = 6 * 256
MAIN_W = LOC_W + Q_LORA + KV_LORA
HALO = 8
POOL_HALF = (1, 2, 4, 8)
LOG2E = 1.4426950408889634

VMEM_LIMIT = 56 * 1024 * 1024

TM_TOKEN = 1024
TM_MERGE = 512
SUB_ROWS = 256
TM_FRONT = 1024
V_ROWS = V_HEAD + 16
TQ_ATTN = 256
SEQS_PER_STEP = 8


def _sel_spec(arr, idx=(), cols=None):
    rest = arr.shape[len(idx):]
    if cols is not None:
        rest = rest[:-1] + (cols,)
    zeros = (0,) * len(rest)
    return pl.BlockSpec((None,) * len(idx) + rest, lambda *_: tuple(idx) + zeros,
                        pipeline_mode=pl.Buffered(1))


def _mods_spec(l, cond0, per_cond):
    return pl.BlockSpec((None, None, N_MOD, D_MODEL), lambda i: (l, cond0 + i // per_cond, 0, 0))


def _params(sem):
    return pltpu.CompilerParams(dimension_semantics=sem, vmem_limit_bytes=VMEM_LIMIT)


def _sigmoid(x):
    return 1.0 / (1.0 + jnp.exp(-x))


def _rms(x):
    return x * lax.rsqrt(jnp.mean(x * x, axis=-1, keepdims=True) + EPS)


def _pre(x, g, mods_ref, s):
    return (_rms(x) * g) * (1.0 + mods_ref[3 * s + 1:3 * s + 2, :]) + mods_ref[3 * s:3 * s + 1, :]


def _dot(a, b):
    return jnp.dot(a, b, preferred_element_type=F32)


def _dot_nt(a, b):
    return lax.dot_general(a, b, (((1,), (1,)), ((), ())), preferred_element_type=F32)


def _mods_kernel(cond_ref, w_ref, b_ref, o_ref):
    c = cond_ref[...]
    a = (c * _sigmoid(c)).astype(BF16)
    o_ref[0] = _dot(a, w_ref[0].astype(BF16)) + b_ref[0]


def _mods_call(cond8, w_mod, b_mod):
    n_tile = 3 * D_MODEL
    nt = (N_MOD * D_MODEL) // n_tile
    return pl.pallas_call(
        _mods_kernel,
        grid=(DEPTH, nt),
        in_specs=[
            pl.BlockSpec((8, D_MODEL), lambda l, j: (0, 0)),
            pl.BlockSpec((1, D_MODEL, n_tile), lambda l, j: (l, 0, j)),
            pl.BlockSpec((1, 1, n_tile), lambda l, j: (l, 0, j)),
        ],
        out_specs=pl.BlockSpec((1, 8, n_tile), lambda l, j: (l, 0, j)),
        out_shape=jax.ShapeDtypeStruct((DEPTH, 8, N_MOD * D_MODEL), F32),
        compiler_params=_params(("parallel", "parallel")),
        name="adaln_mods",
    )(cond8, w_mod, b_mod.reshape(DEPTH, 1, N_MOD * D_MODEL))


def _sub_tiles(tm, sub_rows=SUB_ROWS):
    return [slice(lo, lo + sub_rows) for lo in range(0, tm, sub_rows)]


def _ffn_kernel(x_ref, mods_ref, gpre_ref, gpost_ref, wgu_ref, wdn_ref, o_ref, *, s):
    rows = _sub_tiles(x_ref.shape[0])
    gate = mods_ref[3 * s + 2:3 * s + 3, :]

    def pre(r):
        return _pre(x_ref[rows[r], :], gpre_ref[...], mods_ref, s).astype(BF16)

    def post(r, f):
        o_ref[rows[r], :] = x_ref[rows[r], :] + (0.5 * gate) * (_rms(f) * gpost_ref[...])

    h = pre(0)
    f_prev = None
    for r in range(len(rows)):
        g = _dot(h, wgu_ref[:, :D_FF])
        u = _dot(h, wgu_ref[:, D_FF:])
        a = ((g * _sigmoid(g)) * u).astype(BF16)
        if r + 1 < len(rows):
            h = pre(r + 1)
        f = _dot(a, wdn_ref[...])
        if r > 0:
            post(r - 1, f_prev)
        f_prev = f
    post(len(rows) - 1, f_prev)


def _ffn_call(x, mods, cond, P, l, s):
    rows = x.shape[0]
    tm = TM_TOKEN
    j = s // 2
    return pl.pallas_call(
        functools.partial(_ffn_kernel, s=s),
        grid=(rows // tm,),
        in_specs=[
            pl.BlockSpec((tm, D_MODEL), lambda i: (i, 0)),
            _mods_spec(l, cond[0], rows // cond[1] // tm),
            _sel_spec(P["g_pre"], (l, s)),
            _sel_spec(P["g_post"], (l, s)),
            _sel_spec(P["w_ffn_gu"], (l, j)),
            _sel_spec(P["w_ffn_dn"], (l, j)),
        ],
        out_specs=pl.BlockSpec((tm, D_MODEL), lambda i: (i, 0)),
        out_shape=jax.ShapeDtypeStruct((rows, D_MODEL), F32),
        compiler_params=_params(("parallel",)),
        name="ffn_half_step",
    )(x, mods, P["g_pre"], P["g_post"], P["w_ffn_gu"], P["w_ffn_dn"])


def _front_kernel(*refs, latent):
    if latent:
        (x_ref, mods_ref, gpre_ref, win_ref, gq_ref, wq_ref, gkv_ref, wk_ref, wv_ref,
         cos_ref, sin_ref, cost_ref, sint_ref, zloc_ref, q_ref, k_ref, v_ref) = refs
    else:
        (x_ref, mods_ref, gpre_ref, win_ref, gq_ref, wq_ref, gkv_ref, wk_ref, wv_ref,
         zloc_ref, q_ref, k_ref, v_ref, ckv_ref, kr_ref) = refs
    q_scale = float(QK_NOPE + QK_ROPE) ** -0.5 * LOG2E
    tm = x_ref.shape[0]

    def project(rs, z):
        zloc_ref[rs, :] = z[:, :LOC_W]
        qn = (_rms(z[:, LOC_W:LOC_W + Q_LORA]) * gq_ref[...]).astype(BF16)
        ckv_n = _rms(z[:, LOC_W + Q_LORA:MAIN_W]) * gkv_ref[...]
        kr = z[:, MAIN_W:MAIN_W + HEAD_PAD]
        ckv_b = ckv_n.astype(BF16)
        kn = _dot(ckv_b, wk_ref[...])
        if latent:
            kr = kr * cos_ref[rs, :] + z[:, MAIN_W + HEAD_PAD:] * sin_ref[rs, :]
            vt = _dot_nt(wv_ref[...], ckv_b)
            qt = _dot_nt(wq_ref[0:QK_PAD, :], qn)
            qt_sw = _dot_nt(wq_ref[QK_PAD:, :], qn)
            cost = cost_ref[:, rs]
            sint = sint_ref[:, rs]
            ones = jnp.ones((V_ROWS - V_HEAD, tm), BF16)
            for hd in range(MLA_HEADS):
                sl = slice(hd * HEAD_PAD, (hd + 1) * HEAD_PAD)
                qh = ((qt[sl, :] * cost + qt_sw[sl, :] * sint) * q_scale).astype(BF16)
                for c in range(tm // TQ_ATTN):
                    q_ref[hd, c] = qh[:, c * TQ_ATTN:(c + 1) * TQ_ATTN]
                k_ref[hd, rs, :] = (kn[:, sl] + kr).astype(BF16)
                v_ref[hd, 0:V_HEAD, rs] = vt[hd * V_HEAD:(hd + 1) * V_HEAD, :].astype(BF16)
                v_ref[hd, V_HEAD:, rs] = ones
        else:
            ckv_ref[rs, :] = ckv_n
            kr_ref[rs, :] = kr[:, QK_NOPE:QK_NOPE + QK_ROPE]
            v_ref[rs, :] = _dot(ckv_b, wv_ref[...]).astype(BF16)
            q_ref[rs, :] = (_dot(qn, wq_ref[...]) * q_scale).astype(BF16)
            for hd in range(MLA_HEADS):
                sl = slice(hd * HEAD_PAD, (hd + 1) * HEAD_PAD)
                k_ref[rs, sl] = (kn[:, sl] + kr).astype(BF16)

    h = _pre(x_ref[...], gpre_ref[...], mods_ref, 1).astype(BF16)
    project(slice(0, tm), _dot(h, win_ref[...]))


def _front_call(x, mods, cond, P, l, rope_tabs):
    rows = x.shape[0]
    tm = TM_FRONT
    latent = rope_tabs is not None
    wq = P["wqt"] if latent else P["wq"]
    wv = P["wvt"] if latent else P["wv"]
    row_spec = lambda w: pl.BlockSpec((tm, w), lambda i: (i, 0))
    in_specs = [
        row_spec(D_MODEL),
        _mods_spec(l, cond[0], rows // cond[1] // tm),
        _sel_spec(P["g_pre"], (l, 1)),
        _sel_spec(P["win"], (l,), None if latent else MAIN_W + HEAD_PAD),
        _sel_spec(P["g_q"], (l,)),
        _sel_spec(wq, (l,), None if latent else QK_PAD),
        _sel_spec(P["g_kv"], (l,)),
        _sel_spec(P["wk"], (l,)),
        _sel_spec(wv, (l,)),
    ]
    args = [x, mods, P["g_pre"], P["win"], P["g_q"], wq, P["g_kv"], P["wk"], wv]
    if latent:
        cos_tab, sin_tab, cos_tab_t, sin_tab_t = rope_tabs
        seq_tiles = cos_tab.shape[0] // tm
        in_specs += [pl.BlockSpec((tm, HEAD_PAD), lambda i: (i % seq_tiles, 0))] * 2
        in_specs += [pl.BlockSpec((HEAD_PAD, tm), lambda i: (0, i % seq_tiles))] * 2
        args += [cos_tab, sin_tab, cos_tab_t, sin_tab_t]
        out_specs = [
            row_spec(LOC_W),
            pl.BlockSpec((MLA_HEADS, tm // TQ_ATTN, HEAD_PAD, TQ_ATTN), lambda i: (0, i, 0, 0)),
            pl.BlockSpec((MLA_HEADS, tm, HEAD_PAD), lambda i: (0, i, 0)),
            pl.BlockSpec((MLA_HEADS, V_ROWS, tm), lambda i: (0, 0, i)),
        ]
        out_shape = [
            jax.ShapeDtypeStruct((rows, LOC_W), F32),
            jax.ShapeDtypeStruct((MLA_HEADS, rows // TQ_ATTN, HEAD_PAD, TQ_ATTN), BF16),
            jax.ShapeDtypeStruct((MLA_HEADS, rows, HEAD_PAD), BF16),
            jax.ShapeDtypeStruct((MLA_HEADS, V_ROWS, rows), BF16),
        ]
    else:
        out_specs = [row_spec(LOC_W), row_spec(QK_PAD), row_spec(QK_PAD), row_spec(MLA_W),
                     row_spec(KV_LORA), row_spec(QK_ROPE)]
        out_shape = [
            jax.ShapeDtypeStruct((rows, LOC_W), F32),
            jax.ShapeDtypeStruct((rows, QK_PAD), BF16),
            jax.ShapeDtypeStruct((rows, QK_PAD), BF16),
            jax.ShapeDtypeStruct((rows, MLA_W), BF16),
            jax.ShapeDtypeStruct((rows, KV_LORA), F32),
            jax.ShapeDtypeStruct((rows, QK_ROPE), F32),
        ]
    return pl.pallas_call(
        functools.partial(_front_kernel, latent=latent),
        grid=(rows // tm,),
        in_specs=in_specs,
        out_specs=out_specs,
        out_shape=out_shape,
        compiler_params=_params(("parallel",)),
        name="mix_front_latent" if latent else "mix_front_context",
    )(*args)


def _kvup_kernel(ckv_ref, kr_ref, wk_ref, wvt_ref, place_ref, k_ref, vt_ref):
    ckv_b = ckv_ref[...].astype(BF16)
    kn = _dot(ckv_b, wk_ref[...])
    kr = _dot(kr_ref[...].astype(BF16), place_ref[...])
    vt = _dot_nt(wvt_ref[...], ckv_b)
    ones = jnp.ones((V_ROWS - V_HEAD, ckv_b.shape[0]), BF16)
    for hd in range(MLA_HEADS):
        sl = slice(hd * HEAD_PAD, (hd + 1) * HEAD_PAD)
        k_ref[hd] = (kn[:, sl] + kr).astype(BF16)
        vt_ref[hd, 0:V_HEAD, :] = vt[hd * V_HEAD:(hd + 1) * V_HEAD, :].astype(BF16)
        vt_ref[hd, V_HEAD:, :] = ones


def _kvup_call(cache_ckv, cache_krope, P, l, place):
    b, _, n, _ = cache_ckv.shape
    return pl.pallas_call(
        _kvup_kernel,
        grid=(b,),
        in_specs=[
            pl.BlockSpec((None, None, n, KV_LORA), lambda i: (i, l, 0, 0)),
            pl.BlockSpec((None, None, n, QK_ROPE), lambda i: (i, l, 0, 0)),
            _sel_spec(P["wk"], (l,)),
            _sel_spec(P["wvt"], (l,)),
            _sel_spec(place),
        ],
        out_specs=[pl.BlockSpec((None, MLA_HEADS, n, HEAD_PAD), lambda i: (i, 0, 0, 0)),
                   pl.BlockSpec((None, MLA_HEADS, V_ROWS, n), lambda i: (i, 0, 0, 0))],
        out_shape=[jax.ShapeDtypeStruct((b, MLA_HEADS, n, HEAD_PAD), BF16),
                   jax.ShapeDtypeStruct((b, MLA_HEADS, V_ROWS, n), BF16)],
        compiler_params=_params(("parallel",)),
        name="cache_kv_up",
    )(cache_ckv, cache_krope, P["wk"], P["wvt"], place)


def _by_group(grp, r):
    return jnp.where(grp == 0, r[0:CHUNK],
                     jnp.where(grp == 1, r[CHUNK:2 * CHUNK],
                               jnp.where(grp == 2, r[2 * CHUNK:3 * CHUNK], r[3 * CHUNK:])))


def _attn_kernel(*refs, n_seg):
    qt_ref, qtn_ref = refs[0:2]
    k_refs = refs[2:2 + n_seg]
    vt_refs = refs[2 + n_seg:2 + 2 * n_seg]
    o_ref, sa_ref, sb_ref, ma_ref, mb_ref, ot_ref = refs[2 + 2 * n_seg:]
    buf_a = (sa_ref, ma_ref)
    buf_b = (sb_ref, mb_ref)
    bounds = [0]
    for k_ref in k_refs:
        bounds.append(bounds[-1] + k_ref.shape[1])

    def scores(q_ref, h, buf):
        s_ref, m_ref = buf
        qt = q_ref[h]
        m = None
        for k_ref, lo, hi in zip(k_refs, bounds[:-1], bounds[1:]):
            s = _dot(k_ref[h], qt)
            s_ref[lo:hi, :] = s
            ms = jnp.max(s, axis=0, keepdims=True)
            m = ms if m is None else jnp.maximum(m, ms)
        m_ref[...] = m

    def values(h, buf):
        s_ref, m_ref = buf
        m = m_ref[...]
        ot = None
        for vt_ref, lo, hi in zip(vt_refs, bounds[:-1], bounds[1:]):
            p = jnp.exp2(s_ref[lo:hi, :] - m)
            os_ = _dot(vt_ref[h], p.astype(BF16))
            ot = os_ if ot is None else ot + os_
        ot_ref[pl.ds(pl.multiple_of(h * V_HEAD, V_HEAD), V_HEAD), :] = ot[0:V_HEAD] / ot[V_HEAD:V_HEAD + 1]

    @pl.when(pl.program_id(1) == 0)
    def _():
        scores(qt_ref, 0, buf_a)

    def body(h, carry):
        @pl.when(h % 2 == 0)
        def _():
            scores(qt_ref, h + 1, buf_b)
            values(h, buf_a)

        @pl.when(h % 2 == 1)
        def _():
            scores(qt_ref, h + 1, buf_a)
            values(h, buf_b)
        return carry

    lax.fori_loop(0, MLA_HEADS - 1, body, 0)
    scores(qtn_ref, 0, buf_a)
    values(MLA_HEADS - 1, buf_b)
    o_ref[...] = ot_ref[...].T.astype(BF16)


def _attn_call(qt, segs, batch):
    tq = TQ_ATTN
    rows = qt.shape[1] * tq
    lq = rows // batch
    nq = lq // tq
    n_seg = len(segs)
    k_specs, vt_specs, k_args, vt_args, keys = [], [], [], [], 0
    for k, vt in segs:
        if k.ndim == 4:
            n = k.shape[2]
            k_specs.append(pl.BlockSpec((None, MLA_HEADS, n, HEAD_PAD), lambda i, j: (i, 0, 0, 0)))
            vt_specs.append(pl.BlockSpec((None, MLA_HEADS, V_ROWS, n), lambda i, j: (i, 0, 0, 0)))
        else:
            n = k.shape[1] // batch
            k_specs.append(pl.BlockSpec((MLA_HEADS, n, HEAD_PAD), lambda i, j: (0, i, 0)))
            vt_specs.append(pl.BlockSpec((MLA_HEADS, V_ROWS, n), lambda i, j: (0, 0, i)))
        k_args.append(k)
        vt_args.append(vt)
        keys += n
    assert MLA_HEADS % 2 == 0
    return pl.pallas_call(
        functools.partial(_attn_kernel, n_seg=n_seg),
        grid=(batch, nq),
        in_specs=[
            pl.BlockSpec((MLA_HEADS, None, HEAD_PAD, tq), lambda i, j: (0, i * nq + j, 0, 0)),
            pl.BlockSpec((MLA_HEADS, None, HEAD_PAD, tq),
                         lambda i, j: (0, i * nq + jnp.minimum(j + 1, nq - 1), 0, 0)),
        ] + k_specs + vt_specs,
        out_specs=pl.BlockSpec((tq, MLA_W), lambda i, j: (i * nq + j, 0)),
        out_shape=jax.ShapeDtypeStruct((rows, MLA_W), BF16),
        scratch_shapes=[pltpu.VMEM((keys, tq), F32), pltpu.VMEM((keys, tq), F32),
                        pltpu.VMEM((1, tq), F32), pltpu.VMEM((1, tq), F32),
                        pltpu.VMEM((MLA_W, tq), F32)],
        compiler_params=_params(("arbitrary", "arbitrary")),
        name="mla_attention",
    )(qt, qt, *k_args, *vt_args)


def _attn_short_kernel(q_ref, k_ref, v_ref, o_ref, *, seq):
    lane = lax.broadcasted_iota(jnp.int32, (1, HEAD_PAD), 1)
    for b in range(q_ref.shape[0] // seq):
        rs = slice(b * seq, (b + 1) * seq)
        for j in range(MLA_HEADS // 2):
            vs = v_ref[rs, j * HEAD_PAD:(j + 1) * HEAD_PAD]
            outs = []
            for hd in (2 * j, 2 * j + 1):
                sl = slice(hd * HEAD_PAD, (hd + 1) * HEAD_PAD)
                s = _dot_nt(q_ref[rs, sl], k_ref[rs, sl])
                p = jnp.exp2(s - jnp.max(s, axis=-1, keepdims=True))
                l = jnp.sum(p, axis=-1, keepdims=True)
                outs.append(_dot(p.astype(BF16), vs) / l)
            o_ref[rs, j * HEAD_PAD:(j + 1) * HEAD_PAD] = jnp.where(lane < V_HEAD, outs[0], outs[1]).astype(BF16)


def _attn_short_call(q, k, v, seq):
    rows = q.shape[0]
    tm = SEQS_PER_STEP * seq
    return pl.pallas_call(
        functools.partial(_attn_short_kernel, seq=seq),
        grid=(rows // tm,),
        in_specs=[
            pl.BlockSpec((tm, QK_PAD), lambda i: (i, 0)),
            pl.BlockSpec((tm, QK_PAD), lambda i: (i, 0)),
            pl.BlockSpec((tm, MLA_W), lambda i: (i, 0)),
        ],
        out_specs=pl.BlockSpec((tm, MLA_W), lambda i: (i, 0)),
        out_shape=jax.ShapeDtypeStruct((rows, MLA_W), BF16),
        compiler_params=_params(("parallel",)),
        name="mla_attention_context",
    )(q, k, v)


def _merge_kernel(x_ref, mods_ref, gpre_ref, gpost_ref, z_ref, zprev_ref, znext_ref, att_ref,
                  wpool_ref, pscale_ref, gsgu_ref, wsgu_ref, bsgu_ref, convw_ref, wgate_ref, bgate_ref,
                  wbp_ref, wbs_ref, wbm_ref, wbc_ref, wo_ref, o_ref, ext_ref, *, seqlen):
    tm = x_ref.shape[0]
    rows = _sub_tiles(tm)
    i = pl.program_id(0)
    lane = lax.broadcasted_iota(jnp.int32, (1, POOL_W), 1)
    grp = lane // POOL_GW
    half = jnp.where(grp == 0, POOL_HALF[0],
                     jnp.where(grp == 1, POOL_HALF[1],
                               jnp.where(grp == 2, POOL_HALF[2], POOL_HALF[3])))

    def mixer_in(ref, rs):
        return ref[rs, 0:256], ref[rs, 1024:1280] * ref[rs, 1280:1536]

    def fill(base, rs, before, after):
        n = rs.stop - rs.start
        for c, part in enumerate(zip(before, mixer_in(z_ref, rs), after)):
            cs = slice(c * 256, (c + 1) * 256)
            ext_ref[base:base + HALO, cs] = part[0]
            ext_ref[base + HALO:base + HALO + n, cs] = part[1]
            ext_ref[base + HALO + n:base + 2 * HALO + n, cs] = part[2]

    zero_halo = (jnp.zeros((HALO, 256), F32),) * 2
    per_sequence = seqlen < tm
    if per_sequence:
        assert seqlen == SUB_ROWS
        bases = [r * (SUB_ROWS + CHUNK) for r in range(len(rows))]
    else:
        assert seqlen % tm == 0
        starts_seq = (i * tm) % seqlen == 0
        ends_seq = ((i + 1) * tm) % seqlen == 0
        every = slice(0, HALO)
        fill(0, slice(0, tm),
             tuple(jnp.where(starts_seq, 0.0, v) for v in mixer_in(zprev_ref, every)),
             tuple(jnp.where(ends_seq, 0.0, v) for v in mixer_in(znext_ref, every)))
        bases = [r * SUB_ROWS for r in range(len(rows))]

    def mix(r):
        rs, base = rows[r], bases[r]
        if per_sequence:
            fill(base, rs, zero_halo, zero_halo)
        zp, zu, zv, zb = (z_ref[rs, c * 256:(c + 1) * 256] for c in range(4))
        t = (lax.broadcasted_iota(jnp.int32, (SUB_ROWS, 1), 0) + (i * tm + rs.start)) & (seqlen - 1)
        n_ext = SUB_ROWS + 2 * HALO
        up = lambda v, k: v if k == 0 else pltpu.roll(v, n_ext - k, axis=0)
        a2 = ext_ref[base:base + n_ext, 0:256]
        sums = []
        for w in POOL_HALF:
            a2 = a2 + up(a2, w)
            sums.append(up(a2, HALO - w)[0:SUB_ROWS])
        wsum = jnp.where(grp == 0, sums[0], jnp.where(grp == 1, sums[1], jnp.where(grp == 2, sums[2], sums[3])))
        cnt = (jnp.minimum(t + half, seqlen) - jnp.maximum(t - half, 0)).astype(F32)
        dpool = (wsum / cnt - zp).astype(BF16)
        a = (_dot(dpool, wpool_ref[...]) * pscale_ref[...]).astype(BF16)
        vc = (_rms(zv) * gsgu_ref[...]).astype(BF16)
        parts = []
        for n in range(SUB_ROWS // CHUNK):
            cs = slice(n * CHUNK, (n + 1) * CHUNK)
            mixed = _by_group(grp, _dot(wsgu_ref[...], vc[cs, :]))
            parts.append((zu[cs, :] * (mixed + bsgu_ref[...])).astype(BF16))
        b = jnp.concatenate(parts, axis=0)
        c0 = base + HALO
        conv = (ext_ref[c0 - 1:c0 - 1 + SUB_ROWS, 256:512] * convw_ref[0:1, :]
                + ext_ref[c0:c0 + SUB_ROWS, 256:512] * convw_ref[1:2, :]
                + ext_ref[c0 + 1:c0 + 1 + SUB_ROWS, 256:512] * convw_ref[2:3, :])
        return a, b, (zb * conv).astype(BF16)

    def pre(r):
        return _pre(x_ref[rows[r], :], gpre_ref[...], mods_ref, 1).astype(BF16)

    def post(r, m):
        o_ref[rows[r], :] = x_ref[rows[r], :] + mods_ref[5:6, :] * (_rms(m) * gpost_ref[...])

    h = pre(0)
    loc = mix(0)
    m_prev = None
    for r in range(len(rows)):
        branches = ((loc[0], wbp_ref), (loc[1], wbs_ref), (att_ref[rows[r], :], wbm_ref), (loc[2], wbc_ref))
        merged = None
        for j, (inp, w_ref) in enumerate(branches):
            sl = slice(j * D_MODEL, (j + 1) * D_MODEL)
            gate = _sigmoid(_dot(h, wgate_ref[:, sl]) + bgate_ref[:, sl])
            term = gate * _dot(inp, w_ref[...])
            merged = term if merged is None else merged + term
        if r + 1 < len(rows):
            h = pre(r + 1)
            loc = mix(r + 1)
        m = _dot(merged.astype(BF16), wo_ref[...])
        if r > 0:
            post(r - 1, m_prev)
        m_prev = m
    post(len(rows) - 1, m_prev)


def _merge_call(x, mods, cond, P, l, zloc, att, seqlen):
    rows = x.shape[0]
    tm = TM_MERGE
    n_sub = tm // SUB_ROWS
    ext_rows = n_sub * (SUB_ROWS + CHUNK) if seqlen < tm else tm + CHUNK
    hb = tm // HALO
    last = rows // HALO - 1
    row_spec = lambda w: pl.BlockSpec((tm, w), lambda i: (i, 0))
    local = ("wpool", "pool_scale", "g_sgu", "wsgu", "bsgu", "conv_w")
    names = ("w_gate", "b_gate", "w_br_pool", "w_br_sgu", "w_br_mla", "w_br_conv", "w_o")
    return pl.pallas_call(
        functools.partial(_merge_kernel, seqlen=seqlen),
        grid=(rows // tm,),
        in_specs=[
            row_spec(D_MODEL),
            _mods_spec(l, cond[0], rows // cond[1] // tm),
            _sel_spec(P["g_pre"], (l, 1)),
            _sel_spec(P["g_post"], (l, 1)),
            row_spec(LOC_W),
            pl.BlockSpec((HALO, LOC_W), lambda i: (jnp.maximum(i * hb - 1, 0), 0)),
            pl.BlockSpec((HALO, LOC_W), lambda i: (jnp.minimum((i + 1) * hb, last), 0)),
            row_spec(MLA_W),
        ] + [_sel_spec(P[n], (l,)) for n in local + names],
        out_specs=row_spec(D_MODEL),
        out_shape=jax.ShapeDtypeStruct((rows, D_MODEL), F32),
        scratch_shapes=[pltpu.VMEM((ext_rows, 512), F32)],
        compiler_params=_params(("parallel",)),
        name="gated_merge",
    )(x, mods, P["g_pre"], P["g_post"], zloc, zloc, zloc, att, *[P[n] for n in local + names])


def _rope_swap_perm():
    j = np.arange(QK_ROPE)
    return np.where((j % 16) < 8, j + 8, j - 8)


def _prepare(w_in, w_uq, w_ukv, w_pool, w_sgu, b_sgu):
    perm = _rope_swap_perm()

    zp, zu, zv, cq, ckv, kr, zb, zc, zx = jnp.split(
        w_in, np.cumsum((256, 256, 256, 256, 128, 32, 256, 256)), axis=2)
    pad_rope = lambda a: jnp.pad(a, ((0, 0),) * (a.ndim - 1) + (ROPE_PAD,))
    win = jnp.concatenate([zp, zu, zv, zb, zc, zx, cq, ckv, pad_rope(kr), pad_rope(kr[:, :, perm])],
                          axis=2).astype(BF16)

    uq = w_uq.reshape(DEPTH, Q_LORA, MLA_HEADS, QK_NOPE + QK_ROPE)
    wq = jnp.concatenate(
        [jnp.pad(uq, ((0, 0), (0, 0), (0, 0), (0, HEAD_PAD - QK_NOPE - QK_ROPE))).reshape(DEPTH, Q_LORA, QK_PAD),
         pad_rope(uq[:, :, :, QK_NOPE:][:, :, :, perm]).reshape(DEPTH, Q_LORA, QK_PAD)],
        axis=2).astype(BF16)
    wqt = wq.transpose(0, 2, 1)

    dkv = QK_NOPE + V_HEAD
    wk = jnp.where(jnp.asarray(np.arange(MLA_HEADS * dkv) % dkv < QK_NOPE), w_ukv, 0.0).astype(BF16)
    wv = w_ukv.reshape(DEPTH, KV_LORA, MLA_HEADS, dkv)[:, :, :, QK_NOPE:].reshape(DEPTH, KV_LORA, MLA_W).astype(BF16)
    wvt = wv.transpose(0, 2, 1)

    grp = np.arange(POOL_W) // POOL_GW
    wpool = jnp.where(jnp.asarray(grp[:, None] == grp[None, :]),
                      jnp.tile(w_pool.reshape(DEPTH, POOL_W, POOL_GW), (1, 1, POOL_GROUPS)), 0.0).astype(BF16)
    wsgu = w_sgu.reshape(DEPTH, SGU_GROUPS * CHUNK, CHUNK).astype(BF16)
    bsgu = jnp.repeat(b_sgu.transpose(0, 2, 1), SGU_W // SGU_GROUPS, axis=2)
    return dict(win=win, wq=wq, wqt=wqt, wk=wk, wv=wv, wvt=wvt,
                wpool=wpool, wsgu=wsgu, bsgu=bsgu)


def _rope_tables(n):
    f32 = np.float32
    rows = n // GRID_W
    r = np.repeat(np.arange(rows), GRID_W).astype(f32)
    col = np.tile(np.arange(GRID_W), rows).astype(f32)
    half = QK_ROPE // 2
    freqs = np.power(f32(ROPE_THETA), -(f32(2.0) * np.arange(half // 2, dtype=f32)) / f32(half)).astype(f32)
    ang = np.stack([r[:, None] * freqs, col[:, None] * freqs], axis=1).astype(f32)
    cos, sin = np.cos(ang).astype(f32), np.sin(ang).astype(f32)
    cos32 = np.concatenate([cos, cos], axis=-1).reshape(n, QK_ROPE)
    sin32 = np.concatenate([-sin, sin], axis=-1).reshape(n, QK_ROPE)
    cos_tab = np.concatenate([np.ones((n, QK_NOPE), f32), cos32,
                              np.zeros((n, HEAD_PAD - QK_NOPE - QK_ROPE), f32)], axis=1)
    sin_tab = np.pad(sin32, ((0, 0), ROPE_PAD))
    tabs = (cos_tab, sin_tab, np.ascontiguousarray(cos_tab.T), np.ascontiguousarray(sin_tab.T))
    return tuple(jnp.asarray(t) for t in tabs)


def kernel(x_prompt, x_sample, cache_ckv, cache_krope, c, c_ctx, w_mod, b_mod, g_pre, g_post,
           w_ffn_gu, w_ffn_dn, w_in, w_pool, pool_scale, g_sgu, w_sgu, b_sgu, g_q, w_uq, g_kv,
           w_ukv, conv_w, w_br_pool, w_br_sgu, w_br_mla, w_br_conv, w_gate, b_gate, w_o):
    batch, seq, _ = x_prompt.shape
    dec_batch, dec_seq, _ = x_sample.shape

    cond8 = jnp.concatenate([c_ctx[None, :], c, jnp.zeros((8 - 1 - dec_batch, D_MODEL), F32)], axis=0)
    mods = _mods_call(cond8, w_mod, b_mod).reshape(DEPTH, 8, N_MOD, D_MODEL)
    rope_tabs = _rope_tables(dec_seq)
    place = jnp.asarray(np.pad(np.eye(QK_ROPE, dtype=np.float32), ((0, 0), ROPE_PAD)), dtype=BF16)

    P = _prepare(w_in, w_uq, w_ukv, w_pool, w_sgu, b_sgu)
    row = lambda a: a.reshape(a.shape[:-1] + (1, a.shape[-1]))
    P.update(
        g_pre=row(g_pre), g_post=row(g_post), g_q=row(g_q), g_kv=row(g_kv), g_sgu=row(g_sgu),
        pool_scale=row(pool_scale), b_gate=row(b_gate), conv_w=conv_w,
        w_ffn_gu=w_ffn_gu.astype(BF16), w_ffn_dn=w_ffn_dn.astype(BF16), w_gate=w_gate.astype(BF16),
        w_br_pool=w_br_pool.astype(BF16), w_br_sgu=w_br_sgu.astype(BF16),
        w_br_mla=w_br_mla.astype(BF16), w_br_conv=w_br_conv.astype(BF16), w_o=w_o.astype(BF16),
    )

    streams = {"ctx": (x_prompt.reshape(batch * seq, D_MODEL), (0, 1), batch, seq),
               "lat": (x_sample.reshape(dec_batch * dec_seq, D_MODEL), (1, dec_batch), dec_batch, dec_seq)}
    outs = {}
    ckv_states, kr_states = [], []
    for name, (x, cond, nb, ns) in streams.items():
        is_ctx = name == "ctx"
        for l in range(DEPTH):
            x = _ffn_call(x, mods, cond, P, l, 0)
            front = _front_call(x, mods, cond, P, l, None if is_ctx else rope_tabs)
            zloc, q, k, v = front[:4]
            if is_ctx:
                ckv_states.append(front[4].reshape(batch, seq, KV_LORA))
                kr_states.append(front[5].reshape(batch, seq, QK_ROPE))
                att = _attn_short_call(q, k, v, ns)
            else:
                att = _attn_call(q, [_kvup_call(cache_ckv, cache_krope, P, l, place), (k, v)], nb)
            x = _merge_call(x, mods, cond, P, l, zloc, att, ns)
            x = _ffn_call(x, mods, cond, P, l, 2)
        outs[name] = x

    y_prompt = outs["ctx"].reshape(batch, seq, D_MODEL)
    y_sample = outs["lat"].reshape(dec_batch, dec_seq, D_MODEL)
    state_ckv = jnp.stack(ckv_states, axis=1)
    state_krope = jnp.stack(kr_states, axis=1)
    return (y_prompt, y_sample, state_ckv, state_krope)
```

```python
import functools

import numpy as np
import jax
import jax.numpy as jnp
from jax import lax
from jax.experimental import pallas as pl
from jax.experimental.pallas import tpu as pltpu

F32 = jnp.float32
BF16 = jnp.bfloat16

D_MODEL = 1024
DEPTH = 2
GRID_W = 64
N_MOD = 9
D_FF = 2816
EPS = 1e-6
POOL_GROUPS = 4
POOL_GW = 64
POOL_W = 256
SGU_GROUPS = 4
SGU_W = 256
CHUNK = 128
MLA_HEADS = 8
QK_NOPE = 64
QK_ROPE = 32
V_HEAD = 64
Q_LORA = 256
KV_LORA = 128
MLA_W = MLA_HEADS * V_HEAD
ROPE_THETA = 10000.0
CONV_W = 256
CONV_K = 3
N_BRANCH = 4
HEAD_PAD = 128
QK_PAD = MLA_HEADS * HEAD_PAD
ROPE_PAD = (QK_NOPE, HEAD_PAD - QK_NOPE - QK_ROPE)
LOC_W = 6 * 256
MAIN_W = LOC_W + Q_LORA + KV_LORA
HALO = 8
POOL_HALF = (1, 2, 4, 8)
LOG2E = 1.4426950408889634

VMEM_LIMIT = 56 * 1024 * 1024

TM_TOKEN = 1024
TM_MERGE = 512
SUB_ROWS = 256
TM_FRONT = 1024
V_ROWS = V_HEAD + 16
TQ_ATTN = 256
Q_TILES_PER_STEP = 2
SEQS_PER_STEP = 8


def _sel_spec(arr, idx=(), cols=None):
    rest = arr.shape[len(idx):]
    if cols is not None:
        rest = rest[:-1] + (cols,)
    zeros = (0,) * len(rest)
    return pl.BlockSpec((None,) * len(idx) + rest, lambda *_: tuple(idx) + zeros,
                        pipeline_mode=pl.Buffered(1))


def _mods_spec(l, cond0, per_cond):
    return pl.BlockSpec((None, None, N_MOD, D_MODEL), lambda i: (l, cond0 + i // per_cond, 0, 0))


def _params(sem):
    return pltpu.CompilerParams(dimension_semantics=sem, vmem_limit_bytes=VMEM_LIMIT)


def _sigmoid(x):
    return 1.0 / (1.0 + jnp.exp(-x))


def _rms(x):
    return x * lax.rsqrt(jnp.mean(x * x, axis=-1, keepdims=True) + EPS)


def _pre(x, g, mods_ref, s):
    return (_rms(x) * g) * (1.0 + mods_ref[3 * s + 1:3 * s + 2, :]) + mods_ref[3 * s:3 * s + 1, :]


def _dot(a, b):
    return jnp.dot(a, b, preferred_element_type=F32)


def _dot_nt(a, b):
    return lax.dot_general(a, b, (((1,), (1,)), ((), ())), preferred_element_type=F32)


def _mods_kernel(cond_ref, w_ref, b_ref, o_ref):
    c = cond_ref[...]
    a = (c * _sigmoid(c)).astype(BF16)
    o_ref[0] = _dot(a, w_ref[0].astype(BF16)) + b_ref[0]


def _mods_call(cond8, w_mod, b_mod):
    n_tile = 3 * D_MODEL
    nt = (N_MOD * D_MODEL) // n_tile
    return pl.pallas_call(
        _mods_kernel,
        grid=(DEPTH, nt),
        in_specs=[
            pl.BlockSpec((8, D_MODEL), lambda l, j: (0, 0)),
            pl.BlockSpec((1, D_MODEL, n_tile), lambda l, j: (l, 0, j)),
            pl.BlockSpec((1, 1, n_tile), lambda l, j: (l, 0, j)),
        ],
        out_specs=pl.BlockSpec((1, 8, n_tile), lambda l, j: (l, 0, j)),
        out_shape=jax.ShapeDtypeStruct((DEPTH, 8, N_MOD * D_MODEL), F32),
        compiler_params=_params(("parallel", "parallel")),
        name="adaln_mods",
    )(cond8, w_mod, b_mod.reshape(DEPTH, 1, N_MOD * D_MODEL))


def _sub_tiles(tm, sub_rows=SUB_ROWS):
    return [slice(lo, lo + sub_rows) for lo in range(0, tm, sub_rows)]


def _ffn_kernel(x_ref, mods_ref, gpre_ref, gpost_ref, wgu_ref, wdn_ref, o_ref, *, s):
    rows = _sub_tiles(x_ref.shape[0])
    gate = mods_ref[3 * s + 2:3 * s + 3, :]

    def pre(r):
        return _pre(x_ref[rows[r], :], gpre_ref[...], mods_ref, s).astype(BF16)

    def post(r, f):
        o_ref[rows[r], :] = x_ref[rows[r], :] + (0.5 * gate) * (_rms(f) * gpost_ref[...])

    h = pre(0)
    f_prev = None
    for r in range(len(rows)):
        g = _dot(h, wgu_ref[:, :D_FF])
        u = _dot(h, wgu_ref[:, D_FF:])
        a = ((g * _sigmoid(g)) * u).astype(BF16)
        if r + 1 < len(rows):
            h = pre(r + 1)
        f = _dot(a, wdn_ref[...])
        if r > 0:
            post(r - 1, f_prev)
        f_prev = f
    post(len(rows) - 1, f_prev)


def _ffn_call(x, mods, cond, P, l, s):
    rows = x.shape[0]
    tm = TM_TOKEN
    j = s // 2
    return pl.pallas_call(
        functools.partial(_ffn_kernel, s=s),
        grid=(rows // tm,),
        in_specs=[
            pl.BlockSpec((tm, D_MODEL), lambda i: (i, 0)),
            _mods_spec(l, cond[0], rows // cond[1] // tm),
            _sel_spec(P["g_pre"], (l, s)),
            _sel_spec(P["g_post"], (l, s)),
            _sel_spec(P["w_ffn_gu"], (l, j)),
            _sel_spec(P["w_ffn_dn"], (l, j)),
        ],
        out_specs=pl.BlockSpec((tm, D_MODEL), lambda i: (i, 0)),
        out_shape=jax.ShapeDtypeStruct((rows, D_MODEL), F32),
        compiler_params=_params(("parallel",)),
        name="ffn_half_step",
    )(x, mods, P["g_pre"], P["g_post"], P["w_ffn_gu"], P["w_ffn_dn"])


def _front_kernel(*refs, latent):
    if latent:
        (x_ref, mods_ref, gpre_ref, win_ref, gq_ref, wq_ref, gkv_ref, wk_ref, wv_ref,
         cos_ref, sin_ref, cost_ref, sint_ref, zloc_ref, q_ref, k_ref, v_ref) = refs
    else:
        (x_ref, mods_ref, gpre_ref, win_ref, gq_ref, wq_ref, gkv_ref, wk_ref, wv_ref,
         zloc_ref, q_ref, k_ref, v_ref, ckv_ref, kr_ref) = refs
    q_scale = float(QK_NOPE + QK_ROPE) ** -0.5 * LOG2E
    tm = x_ref.shape[0]

    def project(rs, z):
        zloc_ref[rs, :] = z[:, :LOC_W]
        qn = (_rms(z[:, LOC_W:LOC_W + Q_LORA]) * gq_ref[...]).astype(BF16)
        ckv_n = _rms(z[:, LOC_W + Q_LORA:MAIN_W]) * gkv_ref[...]
        kr = z[:, MAIN_W:MAIN_W + HEAD_PAD]
        ckv_b = ckv_n.astype(BF16)
        kn = _dot(ckv_b, wk_ref[...])
        if latent:
            kr = kr * cos_ref[rs, :] + z[:, MAIN_W + HEAD_PAD:] * sin_ref[rs, :]
            vt = _dot_nt(wv_ref[...], ckv_b)
            qt = _dot_nt(wq_ref[0:QK_PAD, :], qn)
            qt_sw = _dot_nt(wq_ref[QK_PAD:, :], qn)
            cost = cost_ref[:, rs]
            sint = sint_ref[:, rs]
            ones = jnp.ones((V_ROWS - V_HEAD, tm), BF16)
            for hd in range(MLA_HEADS):
                sl = slice(hd * HEAD_PAD, (hd + 1) * HEAD_PAD)
                qh = ((qt[sl, :] * cost + qt_sw[sl, :] * sint) * q_scale).astype(BF16)
                for c in range(tm // TQ_ATTN):
                    q_ref[hd, c] = qh[:, c * TQ_ATTN:(c + 1) * TQ_ATTN]
                k_ref[hd, rs, :] = (kn[:, sl] + kr).astype(BF16)
                v_ref[hd, 0:V_HEAD, rs] = vt[hd * V_HEAD:(hd + 1) * V_HEAD, :].astype(BF16)
                v_ref[hd, V_HEAD:, rs] = ones
        else:
            ckv_ref[rs, :] = ckv_n
            kr_ref[rs, :] = kr[:, QK_NOPE:QK_NOPE + QK_ROPE]
            v_ref[rs, :] = _dot(ckv_b, wv_ref[...]).astype(BF16)
            q_ref[rs, :] = (_dot(qn, wq_ref[...]) * q_scale).astype(BF16)
            for hd in range(MLA_HEADS):
                sl = slice(hd * HEAD_PAD, (hd + 1) * HEAD_PAD)
                k_ref[rs, sl] = (kn[:, sl] + kr).astype(BF16)

    h = _pre(x_ref[...], gpre_ref[...], mods_ref, 1).astype(BF16)
    project(slice(0, tm), _dot(h, win_ref[...]))


def _front_call(x, mods, cond, P, l, rope_tabs):
    rows = x.shape[0]
    tm = TM_FRONT
    latent = rope_tabs is not None
    wq = P["wqt"] if latent else P["wq"]
    wv = P["wvt"] if latent else P["wv"]
    row_spec = lambda w: pl.BlockSpec((tm, w), lambda i: (i, 0))
    in_specs = [
        row_spec(D_MODEL),
        _mods_spec(l, cond[0], rows // cond[1] // tm),
        _sel_spec(P["g_pre"], (l, 1)),
        _sel_spec(P["win"], (l,), None if latent else MAIN_W + HEAD_PAD),
        _sel_spec(P["g_q"], (l,)),
        _sel_spec(wq, (l,), None if latent else QK_PAD),
        _sel_spec(P["g_kv"], (l,)),
        _sel_spec(P["wk"], (l,)),
        _sel_spec(wv, (l,)),
    ]
    args = [x, mods, P["g_pre"], P["win"], P["g_q"], wq, P["g_kv"], P["wk"], wv]
    if latent:
        cos_tab, sin_tab, cos_tab_t, sin_tab_t = rope_tabs
        seq_tiles = cos_tab.shape[0] // tm
        in_specs += [pl.BlockSpec((tm, HEAD_PAD), lambda i: (i % seq_tiles, 0))] * 2
        in_specs += [pl.BlockSpec((HEAD_PAD, tm), lambda i: (0, i % seq_tiles))] * 2
        args += [cos_tab, sin_tab, cos_tab_t, sin_tab_t]
        out_specs = [
            row_spec(LOC_W),
            pl.BlockSpec((MLA_HEADS, tm // TQ_ATTN, HEAD_PAD, TQ_ATTN), lambda i: (0, i, 0, 0)),
            pl.BlockSpec((MLA_HEADS, tm, HEAD_PAD), lambda i: (0, i, 0)),
            pl.BlockSpec((MLA_HEADS, V_ROWS, tm), lambda i: (0, 0, i)),
        ]
        out_shape = [
            jax.ShapeDtypeStruct((rows, LOC_W), F32),
            jax.ShapeDtypeStruct((MLA_HEADS, rows // TQ_ATTN, HEAD_PAD, TQ_ATTN), BF16),
            jax.ShapeDtypeStruct((MLA_HEADS, rows, HEAD_PAD), BF16),
            jax.ShapeDtypeStruct((MLA_HEADS, V_ROWS, rows), BF16),
        ]
    else:
        out_specs = [row_spec(LOC_W), row_spec(QK_PAD), row_spec(QK_PAD), row_spec(MLA_W),
                     row_spec(KV_LORA), row_spec(QK_ROPE)]
        out_shape = [
            jax.ShapeDtypeStruct((rows, LOC_W), F32),
            jax.ShapeDtypeStruct((rows, QK_PAD), BF16),
            jax.ShapeDtypeStruct((rows, QK_PAD), BF16),
            jax.ShapeDtypeStruct((rows, MLA_W), BF16),
            jax.ShapeDtypeStruct((rows, KV_LORA), F32),
            jax.ShapeDtypeStruct((rows, QK_ROPE), F32),
        ]
    return pl.pallas_call(
        functools.partial(_front_kernel, latent=latent),
        grid=(rows // tm,),
        in_specs=in_specs,
        out_specs=out_specs,
        out_shape=out_shape,
        compiler_params=_params(("parallel",)),
        name="mix_front_latent" if latent else "mix_front_context",
    )(*args)


def _kvup_kernel(ckv_ref, kr_ref, wk_ref, wvt_ref, place_ref, k_ref, vt_ref):
    ckv_b = ckv_ref[...].astype(BF16)
    kn = _dot(ckv_b, wk_ref[...])
    kr = _dot(kr_ref[...].astype(BF16), place_ref[...])
    vt = _dot_nt(wvt_ref[...], ckv_b)
    ones = jnp.ones((V_ROWS - V_HEAD, ckv_b.shape[0]), BF16)
    for hd in range(MLA_HEADS):
        sl = slice(hd * HEAD_PAD, (hd + 1) * HEAD_PAD)
        k_ref[hd] = (kn[:, sl] + kr).astype(BF16)
        vt_ref[hd, 0:V_HEAD, :] = vt[hd * V_HEAD:(hd + 1) * V_HEAD, :].astype(BF16)
        vt_ref[hd, V_HEAD:, :] = ones


def _kvup_call(cache_ckv, cache_krope, P, l, place):
    b, _, n, _ = cache_ckv.shape
    return pl.pallas_call(
        _kvup_kernel,
        grid=(b,),
        in_specs=[
            pl.BlockSpec((None, None, n, KV_LORA), lambda i: (i, l, 0, 0)),
            pl.BlockSpec((None, None, n, QK_ROPE), lambda i: (i, l, 0, 0)),
            _sel_spec(P["wk"], (l,)),
            _sel_spec(P["wvt"], (l,)),
            _sel_spec(place),
        ],
        out_specs=[pl.BlockSpec((None, MLA_HEADS, n, HEAD_PAD), lambda i: (i, 0, 0, 0)),
                   pl.BlockSpec((None, MLA_HEADS, V_ROWS, n), lambda i: (i, 0, 0, 0))],
        out_shape=[jax.ShapeDtypeStruct((b, MLA_HEADS, n, HEAD_PAD), BF16),
                   jax.ShapeDtypeStruct((b, MLA_HEADS, V_ROWS, n), BF16)],
        compiler_params=_params(("parallel",)),
        name="cache_kv_up",
    )(cache_ckv, cache_krope, P["wk"], P["wvt"], place)


def _by_group(grp, r):
    return jnp.where(grp == 0, r[0:CHUNK],
                     jnp.where(grp == 1, r[CHUNK:2 * CHUNK],
                               jnp.where(grp == 2, r[2 * CHUNK:3 * CHUNK], r[3 * CHUNK:])))


def _attn_kernel(*refs, n_seg):
    qt_ref, qtn_ref = refs[0:2]
    k_refs = refs[2:2 + n_seg]
    vt_refs = refs[2 + n_seg:2 + 2 * n_seg]
    o_ref, sa_ref, sb_ref, ma_ref, mb_ref, ot_ref = refs[2 + 2 * n_seg:]
    buf_a = (sa_ref, ma_ref)
    buf_b = (sb_ref, mb_ref)
    bounds = [0]
    for k_ref in k_refs:
        bounds.append(bounds[-1] + k_ref.shape[1])

    def scores(qt, h, buf):
        s_ref, m_ref = buf
        m = None
        for k_ref, lo, hi in zip(k_refs, bounds[:-1], bounds[1:]):
            s = _dot(k_ref[h], qt)
            s_ref[lo:hi, :] = s
            ms = jnp.max(s, axis=0, keepdims=True)
            m = ms if m is None else jnp.maximum(m, ms)
        m_ref[...] = m

    def values(h, buf):
        s_ref, m_ref = buf
        m = m_ref[...]
        ot = None
        for vt_ref, lo, hi in zip(vt_refs, bounds[:-1], bounds[1:]):
            p = jnp.exp2(s_ref[lo:hi, :] - m)
            os_ = _dot(vt_ref[h], p.astype(BF16))
            ot = os_ if ot is None else ot + os_
        ot_ref[pl.ds(pl.multiple_of(h * V_HEAD, V_HEAD), V_HEAD), :] = ot[0:V_HEAD] / ot[V_HEAD:V_HEAD + 1]

    @pl.when(pl.program_id(1) == 0)
    def _():
        scores(qt_ref[0, 0], 0, buf_a)

    n_tiles = qt_ref.shape[1]
    tq = qt_ref.shape[3]
    for t in range(n_tiles):
        def body(h, carry, t=t):
            @pl.when(h % 2 == 0)
            def _():
                scores(qt_ref[h + 1, t], h + 1, buf_b)
                values(h, buf_a)

            @pl.when(h % 2 == 1)
            def _():
                scores(qt_ref[h + 1, t], h + 1, buf_a)
                values(h, buf_b)
            return carry

        lax.fori_loop(0, MLA_HEADS - 1, body, 0)
        scores(qt_ref[0, t + 1] if t + 1 < n_tiles else qtn_ref[0], 0, buf_a)
        values(MLA_HEADS - 1, buf_b)
        o_ref[t * tq:(t + 1) * tq, :] = ot_ref[...].T.astype(BF16)


def _attn_call(qt, segs, batch):
    tq = TQ_ATTN
    rows = qt.shape[1] * tq
    lq = rows // batch
    nq = lq // tq
    n_seg = len(segs)
    k_specs, vt_specs, k_args, vt_args, keys = [], [], [], [], 0
    for k, vt in segs:
        if k.ndim == 4:
            n = k.shape[2]
            k_specs.append(pl.BlockSpec((None, MLA_HEADS, n, HEAD_PAD), lambda i, j: (i, 0, 0, 0)))
            vt_specs.append(pl.BlockSpec((None, MLA_HEADS, V_ROWS, n), lambda i, j: (i, 0, 0, 0)))
        else:
            n = k.shape[1] // batch
            k_specs.append(pl.BlockSpec((MLA_HEADS, n, HEAD_PAD), lambda i, j: (0, i, 0)))
            vt_specs.append(pl.BlockSpec((MLA_HEADS, V_ROWS, n), lambda i, j: (0, 0, i)))
        k_args.append(k)
        vt_args.append(vt)
        keys += n
    assert MLA_HEADS % 2 == 0
    tiles = Q_TILES_PER_STEP
    steps = nq // tiles
    return pl.pallas_call(
        functools.partial(_attn_kernel, n_seg=n_seg),
        grid=(batch, steps),
        in_specs=[
            pl.BlockSpec((MLA_HEADS, tiles, HEAD_PAD, tq), lambda i, j: (0, i * steps + j, 0, 0)),
            pl.BlockSpec((MLA_HEADS, None, HEAD_PAD, tq),
                         lambda i, j: (0, i * nq + jnp.minimum((j + 1) * tiles, nq - 1), 0, 0)),
        ] + k_specs + vt_specs,
        out_specs=pl.BlockSpec((tiles * tq, MLA_W), lambda i, j: (i * steps + j, 0)),
        out_shape=jax.ShapeDtypeStruct((rows, MLA_W), BF16),
        scratch_shapes=[pltpu.VMEM((keys, tq), F32), pltpu.VMEM((keys, tq), F32),
                        pltpu.VMEM((1, tq), F32), pltpu.VMEM((1, tq), F32),
                        pltpu.VMEM((MLA_W, tq), F32)],
        compiler_params=_params(("arbitrary", "arbitrary")),
        name="mla_attention",
    )(qt, qt, *k_args, *vt_args)


def _attn_short_kernel(q_ref, k_ref, v_ref, o_ref, *, seq):
    lane = lax.broadcasted_iota(jnp.int32, (1, HEAD_PAD), 1)
    for b in range(q_ref.shape[0] // seq):
        rs = slice(b * seq, (b + 1) * seq)
        for j in range(MLA_HEADS // 2):
            vs = v_ref[rs, j * HEAD_PAD:(j + 1) * HEAD_PAD]
            outs = []
            for hd in (2 * j, 2 * j + 1):
                sl = slice(hd * HEAD_PAD, (hd + 1) * HEAD_PAD)
                s = _dot_nt(q_ref[rs, sl], k_ref[rs, sl])
                p = jnp.exp2(s - jnp.max(s, axis=-1, keepdims=True))
                l = jnp.sum(p, axis=-1, keepdims=True)
                outs.append(_dot(p.astype(BF16), vs) / l)
            o_ref[rs, j * HEAD_PAD:(j + 1) * HEAD_PAD] = jnp.where(lane < V_HEAD, outs[0], outs[1]).astype(BF16)


def _attn_short_call(q, k, v, seq):
    rows = q.shape[0]
    tm = SEQS_PER_STEP * seq
    return pl.pallas_call(
        functools.partial(_attn_short_kernel, seq=seq),
        grid=(rows // tm,),
        in_specs=[
            pl.BlockSpec((tm, QK_PAD), lambda i: (i, 0)),
            pl.BlockSpec((tm, QK_PAD), lambda i: (i, 0)),
            pl.BlockSpec((tm, MLA_W), lambda i: (i, 0)),
        ],
        out_specs=pl.BlockSpec((tm, MLA_W), lambda i: (i, 0)),
        out_shape=jax.ShapeDtypeStruct((rows, MLA_W), BF16),
        compiler_params=_params(("parallel",)),
        name="mla_attention_context",
    )(q, k, v)


def _merge_kernel(x_ref, mods_ref, gpre_ref, gpost_ref, z_ref, zprev_ref, znext_ref, att_ref,
                  wpool_ref, pscale_ref, gsgu_ref, wsgu_ref, bsgu_ref, convw_ref, wgate_ref, bgate_ref,
                  wbp_ref, wbs_ref, wbm_ref, wbc_ref, wo_ref, o_ref, ext_ref, *, seqlen):
    tm = x_ref.shape[0]
    rows = _sub_tiles(tm)
    i = pl.program_id(0)
    lane = lax.broadcasted_iota(jnp.int32, (1, POOL_W), 1)
    grp = lane // POOL_GW
    half = jnp.where(grp == 0, POOL_HALF[0],
                     jnp.where(grp == 1, POOL_HALF[1],
                               jnp.where(grp == 2, POOL_HALF[2], POOL_HALF[3])))

    def mixer_in(ref, rs):
        return ref[rs, 0:256], ref[rs, 1024:1280] * ref[rs, 1280:1536]

    def fill(base, rs, before, after):
        n = rs.stop - rs.start
        for c, part in enumerate(zip(before, mixer_in(z_ref, rs), after)):
            cs = slice(c * 256, (c + 1) * 256)
            ext_ref[base:base + HALO, cs] = part[0]
            ext_ref[base + HALO:base + HALO + n, cs] = part[1]
            ext_ref[base + HALO + n:base + 2 * HALO + n, cs] = part[2]

    zero_halo = (jnp.zeros((HALO, 256), F32),) * 2
    per_sequence = seqlen < tm
    if per_sequence:
        assert seqlen == SUB_ROWS
        bases = [r * (SUB_ROWS + CHUNK) for r in range(len(rows))]
    else:
        assert seqlen % tm == 0
        starts_seq = (i * tm) % seqlen == 0
        ends_seq = ((i + 1) * tm) % seqlen == 0
        every = slice(0, HALO)
        fill(0, slice(0, tm),
             tuple(jnp.where(starts_seq, 0.0, v) for v in mixer_in(zprev_ref, every)),
             tuple(jnp.where(ends_seq, 0.0, v) for v in mixer_in(znext_ref, every)))
        bases = [r * SUB_ROWS for r in range(len(rows))]

    def mix(r):
        rs, base = rows[r], bases[r]
        if per_sequence:
            fill(base, rs, zero_halo, zero_halo)
        zp, zu, zv, zb = (z_ref[rs, c * 256:(c + 1) * 256] for c in range(4))
        t = (lax.broadcasted_iota(jnp.int32, (SUB_ROWS, 1), 0) + (i * tm + rs.start)) & (seqlen - 1)
        n_ext = SUB_ROWS + 2 * HALO
        up = lambda v, k: v if k == 0 else pltpu.roll(v, n_ext - k, axis=0)
        a2 = ext_ref[base:base + n_ext, 0:256]
        sums = []
        for w in POOL_HALF:
            a2 = a2 + up(a2, w)
            sums.append(up(a2, HALO - w)[0:SUB_ROWS])
        wsum = jnp.where(grp == 0, sums[0], jnp.where(grp == 1, sums[1], jnp.where(grp == 2, sums[2], sums[3])))
        cnt = (jnp.minimum(t + half, seqlen) - jnp.maximum(t - half, 0)).astype(F32)
        dpool = (wsum / cnt - zp).astype(BF16)
        a = (_dot(dpool, wpool_ref[...]) * pscale_ref[...]).astype(BF16)
        vc = (_rms(zv) * gsgu_ref[...]).astype(BF16)
        parts = []
        for n in range(SUB_ROWS // CHUNK):
            cs = slice(n * CHUNK, (n + 1) * CHUNK)
            mixed = _by_group(grp, _dot(wsgu_ref[...], vc[cs, :]))
            parts.append((zu[cs, :] * (mixed + bsgu_ref[...])).astype(BF16))
        b = jnp.concatenate(parts, axis=0)
        c0 = base + HALO
        conv = (ext_ref[c0 - 1:c0 - 1 + SUB_ROWS, 256:512] * convw_ref[0:1, :]
                + ext_ref[c0:c0 + SUB_ROWS, 256:512] * convw_ref[1:2, :]
                + ext_ref[c0 + 1:c0 + 1 + SUB_ROWS, 256:512] * convw_ref[2:3, :])
        return a, b, (zb * conv).astype(BF16)

    def pre(r):
        return _pre(x_ref[rows[r], :], gpre_ref[...], mods_ref, 1).astype(BF16)

    def post(r, m):
        o_ref[rows[r], :] = x_ref[rows[r], :] + mods_ref[5:6, :] * (_rms(m) * gpost_ref[...])

    h = pre(0)
    loc = mix(0)
    m_prev = None
    for r in range(len(rows)):
        branches = ((loc[0], wbp_ref), (loc[1], wbs_ref), (att_ref[rows[r], :], wbm_ref), (loc[2], wbc_ref))
        merged = None
        for j, (inp, w_ref) in enumerate(branches):
            sl = slice(j * D_MODEL, (j + 1) * D_MODEL)
            gate = _sigmoid(_dot(h, wgate_ref[:, sl]) + bgate_ref[:, sl])
            term = gate * _dot(inp, w_ref[...])
            merged = term if merged is None else merged + term
        if r + 1 < len(rows):
            h = pre(r + 1)
            loc = mix(r + 1)
        m = _dot(merged.astype(BF16), wo_ref[...])
        if r > 0:
            post(r - 1, m_prev)
        m_prev = m
    post(len(rows) - 1, m_prev)


def _merge_call(x, mods, cond, P, l, zloc, att, seqlen):
    rows = x.shape[0]
    tm = TM_MERGE
    n_sub = tm // SUB_ROWS
    ext_rows = n_sub * (SUB_ROWS + CHUNK) if seqlen < tm else tm + CHUNK
    hb = tm // HALO
    last = rows // HALO - 1
    row_spec = lambda w: pl.BlockSpec((tm, w), lambda i: (i, 0))
    local = ("wpool", "pool_scale", "g_sgu", "wsgu", "bsgu", "conv_w")
    names = ("w_gate", "b_gate", "w_br_pool", "w_br_sgu", "w_br_mla", "w_br_conv", "w_o")
    return pl.pallas_call(
        functools.partial(_merge_kernel, seqlen=seqlen),
        grid=(rows // tm,),
        in_specs=[
            row_spec(D_MODEL),
            _mods_spec(l, cond[0], rows // cond[1] // tm),
            _sel_spec(P["g_pre"], (l, 1)),
            _sel_spec(P["g_post"], (l, 1)),
            row_spec(LOC_W),
            pl.BlockSpec((HALO, LOC_W), lambda i: (jnp.maximum(i * hb - 1, 0), 0)),
            pl.BlockSpec((HALO, LOC_W), lambda i: (jnp.minimum((i + 1) * hb, last), 0)),
            row_spec(MLA_W),
        ] + [_sel_spec(P[n], (l,)) for n in local + names],
        out_specs=row_spec(D_MODEL),
        out_shape=jax.ShapeDtypeStruct((rows, D_MODEL), F32),
        scratch_shapes=[pltpu.VMEM((ext_rows, 512), F32)],
        compiler_params=_params(("parallel",)),
        name="gated_merge",
    )(x, mods, P["g_pre"], P["g_post"], zloc, zloc, zloc, att, *[P[n] for n in local + names])


def _rope_swap_perm():
    j = np.arange(QK_ROPE)
    return np.where((j % 16) < 8, j + 8, j - 8)


def _prepare(w_in, w_uq, w_ukv, w_pool, w_sgu, b_sgu):
    perm = _rope_swap_perm()

    zp, zu, zv, cq, ckv, kr, zb, zc, zx = jnp.split(
        w_in, np.cumsum((256, 256, 256, 256, 128, 32, 256, 256)), axis=2)
    pad_rope = lambda a: jnp.pad(a, ((0, 0),) * (a.ndim - 1) + (ROPE_PAD,))
    win = jnp.concatenate([zp, zu, zv, zb, zc, zx, cq, ckv, pad_rope(kr), pad_rope(kr[:, :, perm])],
                          axis=2).astype(BF16)

    uq = w_uq.reshape(DEPTH, Q_LORA, MLA_HEADS, QK_NOPE + QK_ROPE)
    wq = jnp.concatenate(
        [jnp.pad(uq, ((0, 0), (0, 0), (0, 0), (0, HEAD_PAD - QK_NOPE - QK_ROPE))).reshape(DEPTH, Q_LORA, QK_PAD),
         pad_rope(uq[:, :, :, QK_NOPE:][:, :, :, perm]).reshape(DEPTH, Q_LORA, QK_PAD)],
        axis=2).astype(BF16)
    wqt = wq.transpose(0, 2, 1)

    dkv = QK_NOPE + V_HEAD
    wk = jnp.where(jnp.asarray(np.arange(MLA_HEADS * dkv) % dkv < QK_NOPE), w_ukv, 0.0).astype(BF16)
    wv = w_ukv.reshape(DEPTH, KV_LORA, MLA_HEADS, dkv)[:, :, :, QK_NOPE:].reshape(DEPTH, KV_LORA, MLA_W).astype(BF16)
    wvt = wv.transpose(0, 2, 1)

    grp = np.arange(POOL_W) // POOL_GW
    wpool = jnp.where(jnp.asarray(grp[:, None] == grp[None, :]),
                      jnp.tile(w_pool.reshape(DEPTH, POOL_W, POOL_GW), (1, 1, POOL_GROUPS)), 0.0).astype(BF16)
    wsgu = w_sgu.reshape(DEPTH, SGU_GROUPS * CHUNK, CHUNK).astype(BF16)
    bsgu = jnp.repeat(b_sgu.transpose(0, 2, 1), SGU_W // SGU_GROUPS, axis=2)
    return dict(win=win, wq=wq, wqt=wqt, wk=wk, wv=wv, wvt=wvt,
                wpool=wpool, wsgu=wsgu, bsgu=bsgu)


def _rope_tables(n):
    f32 = np.float32
    rows = n // GRID_W
    r = np.repeat(np.arange(rows), GRID_W).astype(f32)
    col = np.tile(np.arange(GRID_W), rows).astype(f32)
    half = QK_ROPE // 2
    freqs = np.power(f32(ROPE_THETA), -(f32(2.0) * np.arange(half // 2, dtype=f32)) / f32(half)).astype(f32)
    ang = np.stack([r[:, None] * freqs, col[:, None] * freqs], axis=1).astype(f32)
    cos, sin = np.cos(ang).astype(f32), np.sin(ang).astype(f32)
    cos32 = np.concatenate([cos, cos], axis=-1).reshape(n, QK_ROPE)
    sin32 = np.concatenate([-sin, sin], axis=-1).reshape(n, QK_ROPE)
    cos_tab = np.concatenate([np.ones((n, QK_NOPE), f32), cos32,
                              np.zeros((n, HEAD_PAD - QK_NOPE - QK_ROPE), f32)], axis=1)
    sin_tab = np.pad(sin32, ((0, 0), ROPE_PAD))
    tabs = (cos_tab, sin_tab, np.ascontiguousarray(cos_tab.T), np.ascontiguousarray(sin_tab.T))
    return tuple(jnp.asarray(t) for t in tabs)


def kernel(x_prompt, x_sample, cache_ckv, cache_krope, c, c_ctx, w_mod, b_mod, g_pre, g_post,
           w_ffn_gu, w_ffn_dn, w_in, w_pool, pool_scale, g_sgu, w_sgu, b_sgu, g_q, w_uq, g_kv,
           w_ukv, conv_w, w_br_pool, w_br_sgu, w_br_mla, w_br_conv, w_gate, b_gate, w_o):
    batch, seq, _ = x_prompt.shape
    dec_batch, dec_seq, _ = x_sample.shape

    cond8 = jnp.concatenate([c_ctx[None, :], c, jnp.zeros((8 - 1 - dec_batch, D_MODEL), F32)], axis=0)
    mods = _mods_call(cond8, w_mod, b_mod).reshape(DEPTH, 8, N_MOD, D_MODEL)
    rope_tabs = _rope_tables(dec_seq)
    place = jnp.asarray(np.pad(np.eye(QK_ROPE, dtype=np.float32), ((0, 0), ROPE_PAD)), dtype=BF16)

    P = _prepare(w_in, w_uq, w_ukv, w_pool, w_sgu, b_sgu)
    row = lambda a: a.reshape(a.shape[:-1] + (1, a.shape[-1]))
    P.update(
        g_pre=row(g_pre), g_post=row(g_post), g_q=row(g_q), g_kv=row(g_kv), g_sgu=row(g_sgu),
        pool_scale=row(pool_scale), b_gate=row(b_gate), conv_w=conv_w,
        w_ffn_gu=w_ffn_gu.astype(BF16), w_ffn_dn=w_ffn_dn.astype(BF16), w_gate=w_gate.astype(BF16),
        w_br_pool=w_br_pool.astype(BF16), w_br_sgu=w_br_sgu.astype(BF16),
        w_br_mla=w_br_mla.astype(BF16), w_br_conv=w_br_conv.astype(BF16), w_o=w_o.astype(BF16),
    )

    streams = {"ctx": (x_prompt.reshape(batch * seq, D_MODEL), (0, 1), batch, seq),
               "lat": (x_sample.reshape(dec_batch * dec_seq, D_MODEL), (1, dec_batch), dec_batch, dec_seq)}
    outs = {}
    ckv_states, kr_states = [], []
    for name, (x, cond, nb, ns) in streams.items():
        is_ctx = name == "ctx"
        for l in range(DEPTH):
            x = _ffn_call(x, mods, cond, P, l, 0)
            front = _front_call(x, mods, cond, P, l, None if is_ctx else rope_tabs)
            zloc, q, k, v = front[:4]
            if is_ctx:
                ckv_states.append(front[4].reshape(batch, seq, KV_LORA))
                kr_states.append(front[5].reshape(batch, seq, QK_ROPE))
                att = _attn_short_call(q, k, v, ns)
            else:
                att = _attn_call(q, [_kvup_call(cache_ckv, cache_krope, P, l, place), (k, v)], nb)
            x = _merge_call(x, mods, cond, P, l, zloc, att, ns)
            x = _ffn_call(x, mods, cond, P, l, 2)
        outs[name] = x

    y_prompt = outs["ctx"].reshape(batch, seq, D_MODEL)
    y_sample = outs["lat"].reshape(dec_batch, dec_seq, D_MODEL)
    state_ckv = jnp.stack(ckv_states, axis=1)
    state_krope = jnp.stack(kr_states, axis=1)
    return (y_prompt, y_sample, state_ckv, state_krope)
```
